```python
import jax, jax.numpy as jnp
from jax import lax
import numpy as np

D_MODEL = 2048
BATCH = 1
SEQ = 16384
DEPTH = 2

N_A_LAYERS = DEPTH // 2
N_B_LAYERS = DEPTH - N_A_LAYERS
EPS = 1e-6

RET_HEADS = 8
RET_DK = D_MODEL // RET_HEADS
RET_DV = 2 * RET_DK
RET_CHUNK = 128
RET_THETA = 10000.0
RET_IN = 2 * RET_HEADS * RET_DK + 2 * RET_HEADS * RET_DV

DIL_GROUPS = ((128, 1), (512, 4), (2048, 16))
N_GROUPS = len(DIL_GROUPS)
HEAD_DIM = 128
Q_HEADS = D_MODEL // HEAD_DIM
KV_HEADS = 4
ROT_DIM = HEAD_DIM // 4
ROPE_THETA = 500000.0
ATTN_BLOCK = 128
Q_COLS = N_GROUPS * Q_HEADS * HEAD_DIM
KV_COLS = N_GROUPS * 2 * KV_HEADS * HEAD_DIM

FFN_HIDDEN = -(-8 * D_MODEL // (3 * 256)) * 256
N_MOD = 6

kernel_name = "yoco_retention_dilated_attn_adaln"


def rmsnorm(x, g):
    xf = x.astype(jnp.float32)
    y = xf * lax.rsqrt(jnp.mean(xf * xf, axis=-1, keepdims=True) + EPS)
    return (y * g.astype(jnp.float32)).astype(x.dtype)


def modulate(h, shift, scale):
    return h * (1 + scale[:, None, :]) + shift[:, None, :]


def apply_rotary(x, positions, inv_freq):
    rot = 2 * inv_freq.shape[0]
    ang = positions.astype(jnp.float32)[..., None] * inv_freq
    ang = ang.reshape(ang.shape[:2] + (1,) * (x.ndim - 3) + ang.shape[-1:])
    cos, sin = jnp.cos(ang), jnp.sin(ang)
    xr = x[..., :rot].astype(jnp.float32)
    x1, x2 = jnp.split(xr, 2, axis=-1)
    out = jnp.concatenate([x1 * cos - x2 * sin, x2 * cos + x1 * sin], axis=-1).astype(x.dtype)
    return jnp.concatenate([out, x[..., rot:]], axis=-1)


def swiglu(h, w_in, w_out):
    gate, up = jnp.split(h @ w_in, 2, axis=-1)
    return (jax.nn.silu(gate) * up) @ w_out


def chunkwise_retention(q, k, v):
    B, S, H, dk = q.shape
    dv = v.shape[-1]
    n = S // RET_CHUNK
    log_g = jnp.log1p(-(2.0 ** (-5.0 - jnp.arange(H, dtype=jnp.float32))))
    idx = jnp.arange(RET_CHUNK, dtype=jnp.float32)
    diff = idx[:, None] - idx[None, :]
    decay_mask = jnp.where(diff >= 0, jnp.exp(log_g[:, None, None] * jnp.maximum(diff, 0.0)), 0.0)
    q_decay = jnp.exp(log_g[:, None] * (idx + 1.0))
    k_decay = jnp.exp(log_g[:, None] * (RET_CHUNK - 1.0 - idx))
    chunk_decay = jnp.exp(log_g * RET_CHUNK)

    def to_chunks(t):
        return t.astype(jnp.float32).reshape(B, n, RET_CHUNK, H, t.shape[-1]).transpose(1, 0, 3, 2, 4)

    def step(R, xs):
        qc, kc, vc = xs
        scores = jnp.einsum('bhid,bhjd->bhij', qc, kc) * decay_mask
        o = (jnp.einsum('bhij,bhje->bhie', scores, vc)
             + jnp.einsum('bhid,bhde->bhie', qc, R) * q_decay[None, :, :, None])
        R = (R * chunk_decay[None, :, None, None]
             + jnp.einsum('bhjd,bhje->bhde', kc * k_decay[None, :, :, None], vc))
        return R, o

    R0 = jnp.zeros((B, H, dk, dv), jnp.float32)
    _, o = lax.scan(step, R0, (to_chunks(q), to_chunks(k), to_chunks(v)))
    return o.transpose(1, 0, 3, 2, 4).reshape(B, S, H, dv)


def retention_mixer(h, positions, w_in, w_out):
    B, S, _ = h.shape
    hk, hv = RET_HEADS * RET_DK, RET_HEADS * RET_DV
    q, k, v, g = jnp.split(h @ w_in, [hk, 2 * hk, 2 * hk + hv], axis=-1)
    inv_freq = 1.0 / (RET_THETA ** jnp.linspace(0.0, 1.0, RET_DK // 2, dtype=jnp.float32))
    q = apply_rotary(q.reshape(B, S, RET_HEADS, RET_DK), positions, inv_freq)
    k = apply_rotary(k.reshape(B, S, RET_HEADS, RET_DK), positions, inv_freq) * (RET_DK ** -0.5)
    v = v.reshape(B, S, RET_HEADS, RET_DV)
    y = chunkwise_retention(q, k, v)
    y = y * lax.rsqrt(jnp.mean(y * y, axis=-1, keepdims=True) + EPS)
    y = y.reshape(B, S, hv).astype(h.dtype)
    return (jax.nn.silu(g) * y) @ w_out


def shared_kv(x, c, positions, kv_norm_g, kv_ada_w, kv_ada_b, kv_w):
    B, S, _ = x.shape
    shift, scale = jnp.split(jax.nn.silu(c) @ kv_ada_w + kv_ada_b, 2, axis=-1)
    h = modulate(rmsnorm(x, kv_norm_g), shift, scale)
    kv = (h @ kv_w).reshape(B, S, N_GROUPS, 2, KV_HEADS, HEAD_DIM)
    inv_freq = ROPE_THETA ** (-jnp.arange(0, ROT_DIM, 2, dtype=jnp.float32) / ROT_DIM)
    k = apply_rotary(kv[:, :, :, 0], positions, inv_freq)
    return k, kv[:, :, :, 1]


def dilated_group_attention(q, k, v, window, dilation):
    B, S, Hq, Dh = q.shape
    Hkv = k.shape[2]
    rep = Hq // Hkv
    span = dilation * ATTN_BLOCK
    s_pad = -(-S // span) * span
    L = s_pad // dilation
    nb = L // ATTN_BLOCK

    def to_strided(t):
        t = jnp.pad(t.astype(jnp.float32), ((0, 0), (0, s_pad - S), (0, 0), (0, 0)))
        t = t.reshape(B, L, dilation, t.shape[2], Dh).transpose(0, 2, 1, 3, 4)
        return t.reshape(B, dilation, nb, ATTN_BLOCK, t.shape[3], Dh)

    def with_prev(t):
        prev = jnp.concatenate([jnp.zeros_like(t[:, :, :1]), t[:, :, :-1]], axis=2)
        return jnp.concatenate([prev, t], axis=3)

    def from_strided(t):
        perm = (0, 2, 3, 1) + tuple(range(4, t.ndim))
        return t.transpose(perm).reshape((B, s_pad) + t.shape[4:])[:, :S]

    qs = to_strided(q).reshape(B, dilation, nb, ATTN_BLOCK, Hkv, rep, Dh)
    kb = with_prev(to_strided(k))
    vb = with_prev(to_strided(v))
    scores = jnp.einsum('bdnqgrh,bdnkgh->bdngrqk', qs, kb) * (Dh ** -0.5)
    i = jnp.arange(ATTN_BLOCK)[:, None]
    j = jnp.arange(2 * ATTN_BLOCK)[None, :]
    rel = i - j + ATTN_BLOCK
    band = (rel >= 0) & (rel <= window // dilation)
    key_idx = jnp.arange(nb)[:, None] * ATTN_BLOCK - ATTN_BLOCK + jnp.arange(2 * ATTN_BLOCK)[None, :]
    mask = band[None] & (key_idx >= 0)[:, None, :]
    scores = jnp.where(mask[None, None, :, None, None], scores, -jnp.inf)
    lse = jax.nn.logsumexp(scores, axis=-1)
    p = jnp.exp(scores - lse[..., None])
    o = jnp.einsum('bdngrqk,bdnkgh->bdnqgrh', p, vb).reshape(B, dilation, nb, ATTN_BLOCK, Hq, Dh)
    lse = lse.transpose(0, 1, 2, 5, 3, 4).reshape(B, dilation, nb, ATTN_BLOCK, Hq)
    return from_strided(o), from_strided(lse)


def dilated_mixer(h, positions, k_sh, v_sh, w_q, w_out):
    B, S, _ = h.shape
    inv_freq = ROPE_THETA ** (-jnp.arange(0, ROT_DIM, 2, dtype=jnp.float32) / ROT_DIM)
    q = apply_rotary((h @ w_q).reshape(B, S, N_GROUPS, Q_HEADS, HEAD_DIM), positions, inv_freq)
    outs, lses = [], []
    for g, (window, dilation) in enumerate(DIL_GROUPS):
        o, lse = dilated_group_attention(q[:, :, g], k_sh[:, :, g], v_sh[:, :, g], window, dilation)
        outs.append(o)
        lses.append(lse)
    weights = jax.nn.softmax(jnp.stack(lses, axis=0), axis=0)
    o = jnp.sum(weights[..., None] * jnp.stack(outs, axis=0), axis=0)
    return o.reshape(B, S, Q_HEADS * HEAD_DIM).astype(h.dtype) @ w_out


def setup_inputs(seed: int = 0) -> dict:
    key = jax.random.key(seed)
    ks = jax.random.split(key, 20)
    f32 = jnp.float32
    D = D_MODEL

    def w(k, shape, fan_in, gain=1.0):
        return jax.random.normal(k, shape, f32) * (gain * fan_in ** -0.5)

    return {
        "x": jax.random.normal(ks[0], (BATCH, SEQ, D), f32),
        "c": jax.random.normal(ks[1], (BATCH, D), f32),
        "positions": (jax.random.randint(ks[2], (BATCH, 1), 0, 1024, jnp.int32)
                      + jnp.arange(SEQ, dtype=jnp.int32)[None, :]),
        "ada_w": w(ks[3], (DEPTH, D, N_MOD * D), D, 0.5),
        "ada_b": 0.02 * jax.random.normal(ks[4], (DEPTH, N_MOD * D), f32),
        "norm_g": 1.0 + 0.02 * jax.random.normal(ks[5], (DEPTH, 2, D), f32),
        "ffn_w_in": w(ks[6], (DEPTH, D, 2 * FFN_HIDDEN), D),
        "ffn_w_out": w(ks[7], (DEPTH, FFN_HIDDEN, D), FFN_HIDDEN),
        "ret_w_in": w(ks[8], (N_A_LAYERS, D, RET_IN), D),
        "ret_w_out": w(ks[9], (N_A_LAYERS, RET_HEADS * RET_DV, D), RET_HEADS * RET_DV),
        "kv_norm_g": 1.0 + 0.02 * jax.random.normal(ks[10], (D,), f32),
        "kv_ada_w": w(ks[11], (D, 2 * D), D, 0.5),
        "kv_ada_b": 0.02 * jax.random.normal(ks[12], (2 * D,), f32),
        "kv_w": w(ks[13], (D, KV_COLS), D),
        "attn_w_q": w(ks[14], (N_B_LAYERS, D, Q_COLS), D),
        "attn_w_out": w(ks[15], (N_B_LAYERS, Q_HEADS * HEAD_DIM, D), Q_HEADS * HEAD_DIM),
        "final_norm_g": 1.0 + 0.02 * jax.random.normal(ks[16], (D,), f32),
    }


def reference(x, c, positions, ada_w, ada_b, norm_g, ffn_w_in, ffn_w_out, ret_w_in, ret_w_out,
              kv_norm_g, kv_ada_w, kv_ada_b, kv_w, attn_w_q, attn_w_out, final_norm_g):
    c_act = jax.nn.silu(c)
    k_sh, v_sh = None, None
    for layer in range(DEPTH):
        mod = c_act @ ada_w[layer] + ada_b[layer]
        sh_m, sc_m, gt_m, sh_f, sc_f, gt_f = jnp.split(mod, N_MOD, axis=-1)
        h = modulate(rmsnorm(x, norm_g[layer, 0]), sh_m, sc_m)
        if layer < N_A_LAYERS:
            y = retention_mixer(h, positions, ret_w_in[layer], ret_w_out[layer])
        else:
            lb = layer - N_A_LAYERS
            y = dilated_mixer(h, positions, k_sh, v_sh, attn_w_q[lb], attn_w_out[lb])
        x = x + gt_m[:, None, :] * y
        h = modulate(rmsnorm(x, norm_g[layer, 1]), sh_f, sc_f)
        x = x + gt_f[:, None, :] * swiglu(h, ffn_w_in[layer], ffn_w_out[layer])
        if layer == N_A_LAYERS - 1:
            k_sh, v_sh = shared_kv(x, c, positions, kv_norm_g, kv_ada_w, kv_ada_b, kv_w)
    return rmsnorm(x, final_norm_g)
```

```python
import functools

import jax
import jax.numpy as jnp
from jax import lax
from jax.experimental import pallas as pl
from jax.experimental.pallas import tpu as pltpu

F32 = jnp.float32
BF16 = jnp.bfloat16

EPS = 1e-6
RET_HEADS = 8
RET_DK = 256
RET_DV = 512
RET_CHUNK = 128
RET_THETA = 10000.0
DIL_GROUPS = ((128, 1), (512, 4), (2048, 16))
HEAD_DIM = 128
Q_HEADS = 16
KV_HEADS = 4
ROT_DIM = 32
ROPE_THETA = 500000.0
ATTN_BLOCK = 128
N_MOD = 6

LANES = 128
VMEM_LIMIT = 56 * 1024 * 1024


def _params(sem, vmem=VMEM_LIMIT):
    return pltpu.CompilerParams(dimension_semantics=sem, vmem_limit_bytes=vmem)


def _silu(x):
    return x / (1.0 + jnp.exp(-x))


def _matvec_kernel(c_ref, w_ref, b_ref, o_ref):
    c = c_ref[...]
    ca = jnp.broadcast_to(_silu(c), (8, c.shape[1])).astype(BF16)
    acc = jnp.dot(ca, w_ref[...].astype(BF16), preferred_element_type=F32)
    o_ref[...] = acc[0:1, :] + b_ref[...]


def _ada_matvec(c, w, b, tn=1024):
    L, D, N = w.shape
    return pl.pallas_call(
        _matvec_kernel,
        out_shape=jax.ShapeDtypeStruct((L, 1, N), F32),
        grid=(L, N // tn),
        in_specs=[
            pl.BlockSpec((1, D), lambda l, j: (0, 0)),
            pl.BlockSpec((None, D, tn), lambda l, j: (l, 0, j)),
            pl.BlockSpec((None, 1, tn), lambda l, j: (l, 0, j)),
        ],
        out_specs=pl.BlockSpec((None, 1, tn), lambda l, j: (l, 0, j)),
        compiler_params=_params(("parallel", "parallel")),
        name="ada_matvec",
    )(c, w, b)


def _rope_table_kernel(pos_ref, fr_ref, fa_ref, cr_ref, sr_ref, ca_ref, s1_ref, s2_ref):
    pos = pos_ref[...]
    ang_r = pos * fr_ref[...]
    cr_ref[...] = jnp.cos(ang_r)
    sr_ref[...] = jnp.sin(ang_r)
    ang_a = pos * fa_ref[...]
    sa = jnp.sin(ang_a)
    lane = lax.broadcasted_iota(jnp.int32, ang_a.shape, 1)
    ca_ref[...] = jnp.cos(ang_a)
    s1_ref[...] = jnp.where(lane < ROT_DIM // 2, -sa, 0.0)
    s2_ref[...] = jnp.where(lane >= ROT_DIM // 2, sa, 0.0)


def _rope_tables(pos_col, inv_freq_ret, inv_freq_attn, tb=1024):
    S = pos_col.shape[0]
    tab = jax.ShapeDtypeStruct((S, LANES), F32)
    row = pl.BlockSpec((tb, LANES), lambda i: (i, 0))
    vec = pl.BlockSpec((1, LANES), lambda i: (0, 0))
    return pl.pallas_call(
        _rope_table_kernel,
        out_shape=(tab,) * 5,
        grid=(S // tb,),
        in_specs=[pl.BlockSpec((tb, 1), lambda i: (i, 0)), vec, vec],
        out_specs=(row,) * 5,
        compiler_params=_params(("parallel",)),
        name="rope_tables",
    )(pos_col, inv_freq_ret, inv_freq_attn)


NORM_ROWS = 128


def _norm_modulate_into(x_ref, g_ref, sh_ref, sc_ref, h_ref):
    a = g_ref[...] * (1.0 + sc_ref[...])
    b = sh_ref[...]

    def body(c, carry):
        rows = pl.ds(pl.multiple_of(c * NORM_ROWS, NORM_ROWS), NORM_ROWS)
        xc = x_ref[rows, :]
        ms = jnp.mean(xc * xc, axis=-1, keepdims=True)
        h_ref[rows, :] = (xc * lax.rsqrt(ms + EPS) * a + b).astype(BF16)
        return carry

    lax.fori_loop(0, x_ref.shape[0] // NORM_ROWS, body, 0)


def _rope_pair_store(acc, cos, sin, o_ref, scale):
    half = RET_DK // 2
    for hh in range(acc.shape[1] // RET_DK):
        lo = hh * RET_DK
        a1 = acc[:, lo:lo + half]
        a2 = acc[:, lo + half:lo + RET_DK]
        o_ref[:, lo:lo + half] = ((a1 * cos - a2 * sin) * scale).astype(o_ref.dtype)
        o_ref[:, lo + half:lo + RET_DK] = ((a2 * cos + a1 * sin) * scale).astype(o_ref.dtype)


def _rope_partial_store(acc, c, s1, s2, o_ref):
    half = ROT_DIM // 2
    for hh in range(acc.shape[1] // HEAD_DIM):
        lo = hh * HEAD_DIM
        x = acc[:, lo:lo + HEAD_DIM]
        y = x * c + pltpu.roll(x, HEAD_DIM - half, 1) * s1 + pltpu.roll(x, half, 1) * s2
        o_ref[:, lo:lo + HEAD_DIM] = y.astype(o_ref.dtype)


def _ret_in_kernel(x_ref, g_ref, sh_ref, sc_ref, w_ref, cos_ref, sin_ref, o_ref, h_ref, *, n_q, n_k):
    j = pl.program_id(1)

    @pl.when(j == 0)
    def _():
        _norm_modulate_into(x_ref, g_ref, sh_ref, sc_ref, h_ref)

    acc = jnp.dot(h_ref[...], w_ref[...], preferred_element_type=F32)

    @pl.when(j < n_q + n_k)
    def _():
        scale = jnp.where(j >= n_q, RET_DK ** -0.5, 1.0).astype(F32)
        _rope_pair_store(acc, cos_ref[...], sin_ref[...], o_ref, scale)

    @pl.when(j >= n_q + n_k)
    def _():
        o_ref[...] = acc.astype(o_ref.dtype)


def _ret_in_proj(x, g, sh, sc, w, cos, sin, tm=1024, tn=1024):
    S, D = x.shape
    N = w.shape[1]
    hk = RET_HEADS * RET_DK
    vec = pl.BlockSpec((1, D), lambda i, j: (0, 0))
    tab = pl.BlockSpec((tm, LANES), lambda i, j: (i, 0))
    return pl.pallas_call(
        functools.partial(_ret_in_kernel, n_q=hk // tn, n_k=hk // tn),
        out_shape=jax.ShapeDtypeStruct((S, N), BF16),
        grid=(S // tm, N // tn),
        in_specs=[pl.BlockSpec((tm, D), lambda i, j: (i, 0)), vec, vec, vec,
                  pl.BlockSpec((D, tn), lambda i, j: (0, j)), tab, tab],
        out_specs=pl.BlockSpec((tm, tn), lambda i, j: (i, j)),
        scratch_shapes=[pltpu.VMEM((tm, D), BF16)],
        compiler_params=_params(("parallel", "arbitrary")),
        name="ret_in_proj",
    )(x, g, sh, sc, w, cos, sin)


def _attn_proj_kernel(x_ref, g_ref, sh_ref, sc_ref, w_ref, c_ref, s1_ref, s2_ref, o_ref, h_ref,
                      *, out_scale, rope_period):
    j = pl.program_id(1)

    @pl.when(j == 0)
    def _():
        _norm_modulate_into(x_ref, g_ref, sh_ref, sc_ref, h_ref)

    acc = jnp.dot(h_ref[...], w_ref[...], preferred_element_type=F32)

    def rope():
        _rope_partial_store(acc, c_ref[...] * out_scale, s1_ref[...] * out_scale,
                            s2_ref[...] * out_scale, o_ref)

    if rope_period == 1:
        rope()
    else:
        pl.when(j % rope_period == 0)(rope)

        @pl.when(j % rope_period != 0)
        def _():
            o_ref[...] = acc.astype(o_ref.dtype)


def _attn_proj(x, g, sh, sc, w, c, s1, s2, *, out_scale, rope_period, tm=1024, tn=512):
    S, D = x.shape
    N = w.shape[1]
    vec = pl.BlockSpec((1, D), lambda i, j: (0, 0))
    tab = pl.BlockSpec((tm, LANES), lambda i, j: (i, 0))
    return pl.pallas_call(
        functools.partial(_attn_proj_kernel, out_scale=out_scale, rope_period=rope_period),
        out_shape=jax.ShapeDtypeStruct((S, N), BF16),
        grid=(S // tm, N // tn),
        in_specs=[pl.BlockSpec((tm, D), lambda i, j: (i, 0)), vec, vec, vec,
                  pl.BlockSpec((D, tn), lambda i, j: (0, j)), tab, tab, tab],
        out_specs=pl.BlockSpec((tm, tn), lambda i, j: (i, j)),
        scratch_shapes=[pltpu.VMEM((tm, D), BF16)],
        compiler_params=_params(("parallel", "arbitrary")),
        name="attn_proj",
    )(x, g, sh, sc, w, c, s1, s2)


def _ffn_in_kernel(x_ref, g_ref, sh_ref, sc_ref, wg_ref, wu_ref, o_ref, h_ref):
    @pl.when(pl.program_id(1) == 0)
    def _():
        _norm_modulate_into(x_ref, g_ref, sh_ref, sc_ref, h_ref)

    h = h_ref[...]
    gate = jnp.dot(h, wg_ref[...], preferred_element_type=F32)
    up = jnp.dot(h, wu_ref[...], preferred_element_type=F32)
    o_ref[...] = (_silu(gate) * up).astype(o_ref.dtype)


def _ffn_in_proj(x, g, sh, sc, w, tm=1024, tn=512):
    S, D = x.shape
    F = w.shape[1] // 2
    nj = F // tn
    vec = pl.BlockSpec((1, D), lambda i, j: (0, 0))
    return pl.pallas_call(
        _ffn_in_kernel,
        out_shape=jax.ShapeDtypeStruct((S, F), BF16),
        grid=(S // tm, nj),
        in_specs=[pl.BlockSpec((tm, D), lambda i, j: (i, 0)), vec, vec, vec,
                  pl.BlockSpec((D, tn), lambda i, j: (0, j)),
                  pl.BlockSpec((D, tn), lambda i, j: (0, j + nj))],
        out_specs=pl.BlockSpec((tm, tn), lambda i, j: (i, j)),
        scratch_shapes=[pltpu.VMEM((tm, D), BF16)],
        compiler_params=_params(("parallel", "arbitrary")),
        name="ffn_in_proj",
    )(x, g, sh, sc, w, w)


def _out_proj_kernel(a_ref, w_ref, r_ref, gt_ref, o_ref):
    acc = jnp.dot(a_ref[...], w_ref[...], preferred_element_type=F32)
    o_ref[...] = r_ref[...] + gt_ref[...] * acc


def _out_proj(a, w, resid, gate, tm=1024, tn=512):
    S, K = a.shape
    N = w.shape[1]
    return pl.pallas_call(
        _out_proj_kernel,
        out_shape=jax.ShapeDtypeStruct((S, N), F32),
        grid=(S // tm, N // tn),
        in_specs=[pl.BlockSpec((tm, K), lambda i, j: (i, 0)),
                  pl.BlockSpec((K, tn), lambda i, j: (0, j)),
                  pl.BlockSpec((tm, tn), lambda i, j: (i, j)),
                  pl.BlockSpec((1, tn), lambda i, j: (0, j))],
        out_specs=pl.BlockSpec((tm, tn), lambda i, j: (i, j)),
        compiler_params=_params(("parallel", "parallel")),
        name="out_proj",
    )(a, w, resid, gate)


def _attn_out_kernel(o0_ref, o1_ref, o2_ref, l0_ref, l1_ref, l2_ref, w_ref, r_ref, gt_ref, o_ref, a_ref):
    @pl.when(pl.program_id(1) == 0)
    def _():
        l0, l1, l2 = l0_ref[...], l1_ref[...], l2_ref[...]
        mx = jnp.maximum(jnp.maximum(l0, l1), l2)
        e0, e1, e2 = jnp.exp(l0 - mx), jnp.exp(l1 - mx), jnp.exp(l2 - mx)
        inv = 1.0 / (e0 + e1 + e2)
        w0, w1, w2 = e0 * inv, e1 * inv, e2 * inv
        for h in range(Q_HEADS):
            cols = slice(h * HEAD_DIM, (h + 1) * HEAD_DIM)
            m = (w0[:, h:h + 1] * o0_ref[:, cols].astype(F32)
                 + w1[:, h:h + 1] * o1_ref[:, cols].astype(F32)
                 + w2[:, h:h + 1] * o2_ref[:, cols].astype(F32))
            a_ref[:, cols] = m.astype(BF16)

    acc = jnp.dot(a_ref[...], w_ref[...], preferred_element_type=F32)
    o_ref[...] = r_ref[...] + gt_ref[...] * acc


def _attn_out_proj(outs, lses, w, resid, gate, tm=512, tn=1024):
    S, K = outs[0].shape
    N = w.shape[1]
    o_spec = pl.BlockSpec((tm, K), lambda i, j: (i, 0))
    l_spec = pl.BlockSpec((tm, LANES), lambda i, j: (i, 0))
    return pl.pallas_call(
        _attn_out_kernel,
        out_shape=jax.ShapeDtypeStruct((S, N), F32),
        grid=(S // tm, N // tn),
        in_specs=[o_spec, o_spec, o_spec, l_spec, l_spec, l_spec,
                  pl.BlockSpec((K, tn), lambda i, j: (0, j)),
                  pl.BlockSpec((tm, tn), lambda i, j: (i, j)),
                  pl.BlockSpec((1, tn), lambda i, j: (0, j))],
        out_specs=pl.BlockSpec((tm, tn), lambda i, j: (i, j)),
        scratch_shapes=[pltpu.VMEM((tm, K), BF16)],
        compiler_params=_params(("parallel", "arbitrary")),
        name="attn_out_proj",
    )(*outs, *lses, w, resid, gate)


def _retention_kernel(lg_ref, q_ref, k_ref, v_ref, g_ref, o_ref, r_ref):
    h = pl.program_id(0)

    @pl.when(pl.program_id(1) == 0)
    def _():
        r_ref[...] = jnp.zeros_like(r_ref)

    C = RET_CHUNK
    lg = lg_ref[h]
    ri = lax.broadcasted_iota(jnp.int32, (C, C), 0)
    ci = lax.broadcasted_iota(jnp.int32, (C, C), 1)
    diff = (ri - ci).astype(F32)
    dmask = jnp.where(diff >= 0, jnp.exp(lg * jnp.maximum(diff, 0.0)), 0.0)
    idx = lax.broadcasted_iota(jnp.int32, (C, 1), 0).astype(F32)
    q_decay = jnp.exp(lg * (idx + 1.0))
    k_decay = jnp.exp(lg * (C - 1.0 - idx))
    chunk_decay = jnp.exp(jnp.full((1, 1), lg * C, F32))

    for c in range(q_ref.shape[0] // C):
        rows = pl.ds(c * C, C)
        qc = q_ref[rows, :]
        kc = k_ref[rows, :]
        vc = v_ref[rows, :]
        state = r_ref[...]
        scores = lax.dot_general(qc, kc, (((1,), (1,)), ((), ())), preferred_element_type=F32) * dmask
        o = (jnp.dot(scores.astype(BF16), vc, preferred_element_type=F32)
             + jnp.dot(qc, state.astype(BF16), preferred_element_type=F32) * q_decay)
        kd = (kc.astype(F32) * k_decay).astype(BF16)
        r_ref[...] = state * chunk_decay + lax.dot_general(
            kd, vc, (((0,), (0,)), ((), ())), preferred_element_type=F32)
        y = o * lax.rsqrt(jnp.mean(o * o, axis=-1, keepdims=True) + EPS)
        gc = g_ref[rows, :].astype(F32)
        o_ref[rows, :] = (_silu(gc) * y).astype(o_ref.dtype)


def _retention(qkvg, log_g, tb=1024):
    S = qkvg.shape[0]
    kb = RET_HEADS * RET_DK // RET_DK
    vb = 2 * RET_HEADS * RET_DK // RET_DV
    gb = vb + RET_HEADS
    return pl.pallas_call(
        _retention_kernel,
        out_shape=jax.ShapeDtypeStruct((S, RET_HEADS * RET_DV), BF16),
        grid=(RET_HEADS, S // tb),
        in_specs=[pl.BlockSpec(memory_space=pltpu.SMEM),
                  pl.BlockSpec((tb, RET_DK), lambda h, t: (t, h)),
                  pl.BlockSpec((tb, RET_DK), lambda h, t: (t, kb + h)),
                  pl.BlockSpec((tb, RET_DV), lambda h, t: (t, vb + h)),
                  pl.BlockSpec((tb, RET_DV), lambda h, t: (t, gb + h))],
        out_specs=pl.BlockSpec((tb, RET_DV), lambda h, t: (t, h)),
        scratch_shapes=[pltpu.VMEM((RET_DK, RET_DV), F32)],
        compiler_params=_params(("parallel", "arbitrary")),
        name="retention",
    )(log_g, qkvg, qkvg, qkvg, qkvg)


def _dilated_attn_kernel(q_ref, kv_ref, o_ref, lse_ref, prev_ref):
    n = pl.program_id(1)
    B = ATTN_BLOCK
    rep = Q_HEADS // KV_HEADS

    @pl.when(n == 0)
    def _():
        prev_ref[...] = jnp.zeros_like(prev_ref)

    ri = lax.broadcasted_iota(jnp.int32, (rep * B, 2 * B), 0) % B
    ci = lax.broadcasted_iota(jnp.int32, (rep * B, 2 * B), 1)
    rel = ri - ci + B
    valid = (rel >= 0) & (rel <= B) & ((ci >= B) | (n > 0))
    lane = lax.broadcasted_iota(jnp.int32, (B, LANES), 1)
    lse_tile = jnp.zeros((B, LANES), F32)

    for hd in range(KV_HEADS):
        kcols = slice(hd * HEAD_DIM, (hd + 1) * HEAD_DIM)
        vcols = slice((KV_HEADS + hd) * HEAD_DIM, (KV_HEADS + hd + 1) * HEAD_DIM)
        k_cat = jnp.concatenate([prev_ref[:, kcols], kv_ref[:, kcols]], axis=0)
        v_cat = jnp.concatenate([prev_ref[:, vcols], kv_ref[:, vcols]], axis=0)
        q4 = jnp.concatenate(
            [q_ref[:, (hd * rep + r) * HEAD_DIM:(hd * rep + r + 1) * HEAD_DIM] for r in range(rep)], axis=0)
        s = lax.dot_general(q4, k_cat, (((1,), (1,)), ((), ())), preferred_element_type=F32)
        s = jnp.where(valid, s, -jnp.inf)
        m = jnp.max(s, axis=-1, keepdims=True)
        p = jnp.exp(s - m)
        l = jnp.sum(p, axis=-1, keepdims=True)
        o = jnp.dot(p.astype(BF16), v_cat, preferred_element_type=F32) / l
        lse = m + jnp.log(l)
        for r in range(rep):
            qh = hd * rep + r
            o_ref[:, qh * HEAD_DIM:(qh + 1) * HEAD_DIM] = o[r * B:(r + 1) * B, :].astype(o_ref.dtype)
            lse_tile = jnp.where(lane == qh, lse[r * B:(r + 1) * B, :], lse_tile)

    lse_ref[...] = lse_tile
    prev_ref[...] = kv_ref[...]


def _dilated_attention(q_all, kv_all, group, dilation):
    S = q_all.shape[0]
    L = S // dilation
    qw = Q_HEADS * HEAD_DIM
    kvw = 2 * KV_HEADS * HEAD_DIM
    ng = q_all.shape[1] // qw
    qs = q_all.reshape(L, dilation * q_all.shape[1])
    kvs = kv_all.reshape(L, dilation * kv_all.shape[1])
    o, lse = pl.pallas_call(
        _dilated_attn_kernel,
        out_shape=(jax.ShapeDtypeStruct((L, dilation * qw), BF16),
                   jax.ShapeDtypeStruct((L, dilation * LANES), F32)),
        grid=(dilation, L // ATTN_BLOCK),
        in_specs=[pl.BlockSpec((ATTN_BLOCK, qw), lambda r, n: (n, r * ng + group)),
                  pl.BlockSpec((ATTN_BLOCK, kvw), lambda r, n: (n, r * ng + group))],
        out_specs=(pl.BlockSpec((ATTN_BLOCK, qw), lambda r, n: (n, r)),
                   pl.BlockSpec((ATTN_BLOCK, LANES), lambda r, n: (n, r))),
        scratch_shapes=[pltpu.VMEM((ATTN_BLOCK, kvw), BF16)],
        compiler_params=_params(("parallel", "arbitrary")),
        name=f"dilated_attn_g{group}",
    )(qs, kvs)
    return o.reshape(S, qw), lse.reshape(S, LANES)


def _final_norm_kernel(x_ref, g_ref, o_ref):
    x = x_ref[...]
    ms = jnp.mean(x * x, axis=-1, keepdims=True)
    o_ref[...] = x * lax.rsqrt(ms + EPS) * g_ref[...]


def _final_norm(x, g, tm=512):
    S, D = x.shape
    return pl.pallas_call(
        _final_norm_kernel,
        out_shape=jax.ShapeDtypeStruct((S, D), F32),
        grid=(S // tm,),
        in_specs=[pl.BlockSpec((tm, D), lambda i: (i, 0)), pl.BlockSpec((1, D), lambda i: (0, 0))],
        out_specs=pl.BlockSpec((tm, D), lambda i: (i, 0)),
        compiler_params=_params(("parallel",)),
        name="final_norm",
    )(x, g)


def kernel(x, c, positions, ada_w, ada_b, norm_g, ffn_w_in, ffn_w_out, ret_w_in, ret_w_out, kv_norm_g, kv_ada_w, kv_ada_b, kv_w, attn_w_q, attn_w_out, final_norm_g):
    B, S, D = x.shape
    assert B == 1
    xs = x.reshape(S, D)

    mod = _ada_matvec(c, ada_w, ada_b[:, None, :])
    kv_mod = _ada_matvec(c, kv_ada_w[None], kv_ada_b[None, None, :])[0]

    def mod_vec(layer, idx):
        return mod[layer, :, idx * D:(idx + 1) * D]

    inv_freq_ret = 1.0 / (RET_THETA ** jnp.linspace(0.0, 1.0, RET_DK // 2, dtype=F32))
    inv_freq_attn = ROPE_THETA ** (-jnp.arange(0, ROT_DIM, 2, dtype=F32) / ROT_DIM)
    inv_freq_attn = jnp.concatenate(
        [inv_freq_attn, inv_freq_attn, jnp.zeros((HEAD_DIM - ROT_DIM,), F32)])
    pos_col = positions.reshape(S, 1).astype(F32)
    cos_r, sin_r, c_a, s1_a, s2_a = _rope_tables(pos_col, inv_freq_ret[None, :], inv_freq_attn[None, :])

    qkvg = _ret_in_proj(xs, norm_g[0, 0][None], mod_vec(0, 0), mod_vec(0, 1),
                        ret_w_in[0].astype(BF16), cos_r, sin_r)
    log_g = jnp.log1p(-(2.0 ** (-5.0 - jnp.arange(RET_HEADS, dtype=F32))))
    y = _retention(qkvg, log_g)
    xs = _out_proj(y, ret_w_out[0].astype(BF16), xs, mod_vec(0, 2))
    hid = _ffn_in_proj(xs, norm_g[0, 1][None], mod_vec(0, 3), mod_vec(0, 4), ffn_w_in[0].astype(BF16))
    xs = _out_proj(hid, ffn_w_out[0].astype(BF16), xs, mod_vec(0, 5))

    kv_all = _attn_proj(xs, kv_norm_g[None], kv_mod[:, :D], kv_mod[:, D:], kv_w.astype(BF16),
                        c_a, s1_a, s2_a, out_scale=1.0, rope_period=2)

    q_all = _attn_proj(xs, norm_g[1, 0][None], mod_vec(1, 0), mod_vec(1, 1), attn_w_q[0].astype(BF16),
                       c_a, s1_a, s2_a, out_scale=HEAD_DIM ** -0.5, rope_period=1)
    outs, lses = [], []
    for gi, (window, dilation) in enumerate(DIL_GROUPS):
        assert window // dilation == ATTN_BLOCK
        o, lse = _dilated_attention(q_all, kv_all, gi, dilation)
        outs.append(o)
        lses.append(lse)
    xs = _attn_out_proj(outs, lses, attn_w_out[0].astype(BF16), xs, mod_vec(1, 2))
    hid = _ffn_in_proj(xs, norm_g[1, 1][None], mod_vec(1, 3), mod_vec(1, 4), ffn_w_in[1].astype(BF16))
    xs = _out_proj(hid, ffn_w_out[1].astype(BF16), xs, mod_vec(1, 5))

    return _final_norm(xs, final_norm_g[None]).reshape(B, S, D)
```

```python
import functools

import jax
import jax.numpy as jnp
from jax import lax
from jax.experimental import pallas as pl
from jax.experimental.pallas import tpu as pltpu

F32 = jnp.float32
BF16 = jnp.bfloat16

EPS = 1e-6
RET_HEADS = 8
RET_DK = 256
RET_DV = 512
RET_CHUNK = 128
RET_THETA = 10000.0
DIL_GROUPS = ((128, 1), (512, 4), (2048, 16))
HEAD_DIM = 128
Q_HEADS = 16
KV_HEADS = 4
ROT_DIM = 32
ROPE_THETA = 500000.0
ATTN_BLOCK = 128
N_MOD = 6

LANES = 128
VMEM_LIMIT = 56 * 1024 * 1024
SPAN = ATTN_BLOCK * max(d for _, d in DIL_GROUPS)
ROT_PARTNER = LANES // 2
MASKED = -1e30
LOG2E = 1.4426950408889634


def _params(sem, vmem=VMEM_LIMIT):
    return pltpu.CompilerParams(dimension_semantics=sem, vmem_limit_bytes=vmem)


def _silu(x):
    return x / (1.0 + jnp.exp(-x))


def _matvec_kernel(c_ref, w_ref, b_ref, o_ref):
    c = c_ref[...]
    ca = jnp.broadcast_to(_silu(c), (8, c.shape[1])).astype(BF16)
    acc = jnp.dot(ca, w_ref[...].astype(BF16), preferred_element_type=F32)
    o_ref[...] = acc[0:1, :] + b_ref[...]


def _ada_matvec(c, w, b, tn=1024):
    L, D, N = w.shape
    return pl.pallas_call(
        _matvec_kernel,
        out_shape=jax.ShapeDtypeStruct((L, 1, N), F32),
        grid=(L, N // tn),
        in_specs=[
            pl.BlockSpec((1, D), lambda l, j: (0, 0)),
            pl.BlockSpec((None, D, tn), lambda l, j: (l, 0, j)),
            pl.BlockSpec((None, 1, tn), lambda l, j: (l, 0, j)),
        ],
        out_specs=pl.BlockSpec((None, 1, tn), lambda l, j: (l, 0, j)),
        compiler_params=_params(("parallel", "parallel")),
        name="ada_matvec",
    )(c, w, b)


def _rope_table_kernel(pos_ref, fr_ref, fa_ref, cr_ref, sr_ref, ca_ref, sa_ref):
    pos = pos_ref[...]
    ang_r = pos * fr_ref[...]
    cr_ref[...] = jnp.cos(ang_r)
    sr_ref[...] = jnp.sin(ang_r)
    ang_a = pos * fa_ref[...]
    lane = lax.broadcasted_iota(jnp.int32, ang_a.shape, 1)
    ca_ref[...] = jnp.cos(ang_a)
    sa_ref[...] = jnp.where(lane < ROT_PARTNER, -jnp.sin(ang_a), jnp.sin(ang_a))


def _rope_tables(pos_col, inv_freq_ret, inv_freq_attn, tb=1024):
    S = pos_col.shape[0]
    tab = jax.ShapeDtypeStruct((S, LANES), F32)
    row = pl.BlockSpec((tb, LANES), lambda i: (i, 0))
    vec = pl.BlockSpec((1, LANES), lambda i: (0, 0))
    return pl.pallas_call(
        _rope_table_kernel,
        out_shape=(tab,) * 4,
        grid=(S // tb,),
        in_specs=[pl.BlockSpec((tb, 1), lambda i: (i, 0)), vec, vec],
        out_specs=(row,) * 4,
        compiler_params=_params(("parallel",)),
        name="rope_tables",
    )(pos_col, inv_freq_ret, inv_freq_attn)


def _pair_rotary_cols(w):
    half = ROT_DIM // 2
    D, N = w.shape
    wh = w.reshape(D, N // HEAD_DIM, HEAD_DIM)
    wh = jnp.concatenate([wh[..., :half], wh[..., ROT_PARTNER:ROT_PARTNER + half], wh[..., ROT_DIM:ROT_PARTNER],
                          wh[..., half:ROT_DIM], wh[..., ROT_PARTNER + half:]], axis=-1)
    return wh.reshape(D, N)


NORM_ROWS = 128


def _norm_modulate_into(x_ref, g_ref, sh_ref, sc_ref, h_ref):
    a = g_ref[...] * (1.0 + sc_ref[...])
    b = sh_ref[...]

    def body(c, carry):
        rows = pl.ds(pl.multiple_of(c * NORM_ROWS, NORM_ROWS), NORM_ROWS)
        xc = x_ref[rows, :]
        ms = jnp.mean(xc * xc, axis=-1, keepdims=True)
        h_ref[rows, :] = (xc * lax.rsqrt(ms + EPS) * a + b).astype(BF16)
        return carry

    lax.fori_loop(0, x_ref.shape[0] // NORM_ROWS, body, 0)


def _ret_in_kernel(x_ref, g_ref, sh_ref, sc_ref, w_ref, cos_ref, sin_ref, o_ref, h_ref, *, n_q, n_k, n_v):
    j = pl.program_id(1)

    @pl.when(j == 0)
    def _():
        _norm_modulate_into(x_ref, g_ref, sh_ref, sc_ref, h_ref)

    acc = jnp.dot(h_ref[...], w_ref[...], preferred_element_type=F32)

    @pl.when(j < n_q + n_k)
    def _():
        scale = jnp.where(j >= n_q, RET_DK ** -0.5, 1.0).astype(F32)
        cos = cos_ref[...] * scale
        sin = sin_ref[...] * scale
        half = RET_DK // 2
        for hh in range(acc.shape[1] // RET_DK):
            lo = hh * RET_DK
            a1 = acc[:, lo:lo + half]
            a2 = acc[:, lo + half:lo + RET_DK]
            o_ref[:, lo:lo + half] = (a1 * cos - a2 * sin).astype(o_ref.dtype)
            o_ref[:, lo + half:lo + RET_DK] = (a2 * cos + a1 * sin).astype(o_ref.dtype)

    @pl.when((j >= n_q + n_k) & (j < n_q + n_k + n_v))
    def _():
        o_ref[...] = acc.astype(o_ref.dtype)

    @pl.when(j >= n_q + n_k + n_v)
    def _():
        o_ref[...] = _silu(acc).astype(o_ref.dtype)


def _ret_in_proj(x, g, sh, sc, w, cos, sin, tm=1024, tn=1024):
    S, D = x.shape
    N = w.shape[1]
    hk = RET_HEADS * RET_DK
    hv = RET_HEADS * RET_DV
    vec = pl.BlockSpec((1, D), lambda i, j: (0, 0))
    tab = pl.BlockSpec((tm, LANES), lambda i, j: (i, 0))
    return pl.pallas_call(
        functools.partial(_ret_in_kernel, n_q=hk // tn, n_k=hk // tn, n_v=hv // tn),
        out_shape=jax.ShapeDtypeStruct((S, N), BF16),
        grid=(S // tm, N // tn),
        in_specs=[pl.BlockSpec((tm, D), lambda i, j: (i, 0)), vec, vec, vec,
                  pl.BlockSpec((D, tn), lambda i, j: (0, j)), tab, tab],
        out_specs=pl.BlockSpec((tm, tn), lambda i, j: (i, j)),
        scratch_shapes=[pltpu.VMEM((tm, D), BF16)],
        compiler_params=_params(("parallel", "arbitrary")),
        name="ret_in_proj",
    )(x, g, sh, sc, w, cos, sin)


def _span_proj_kernel(x_ref, g_ref, sh_ref, sc_ref, w_ref, c_ref, s_ref, o_ref, h_ref, acc_ref,
                      *, out_scale, tiles_per_group, rope_period):
    j = pl.program_id(1)

    @pl.when(j == 0)
    def _():
        _norm_modulate_into(x_ref, g_ref, sh_ref, sc_ref, h_ref)

    acc = jnp.dot(h_ref[...], w_ref[...], preferred_element_type=F32)
    n_slab = acc.shape[1] // LANES
    for c in range(n_slab):
        acc_ref[c] = acc[:, c * LANES:(c + 1) * LANES]

    def emit(d, rope):
        B = ATTN_BLOCK
        for s in range(SPAN // (B * d)):
            for r in range(d):
                u = s * d + r
                src = pl.ds(s * B * d + r, B, stride=d) if d > 1 else pl.ds(u * B, B)
                dst = pl.ds(u * B, B)
                if rope:
                    ct = c_ref[src, :] * out_scale
                    st = s_ref[src, :] * out_scale
                for c in range(n_slab):
                    xr = acc_ref[c, src, :]
                    y = xr * ct + pltpu.roll(xr, ROT_PARTNER, 1) * st if rope else xr
                    o_ref[dst, c * LANES:(c + 1) * LANES] = y.astype(o_ref.dtype)

    for gi, (_, d) in enumerate(DIL_GROUPS):
        in_group = (j // tiles_per_group) == gi
        if rope_period == 1:
            pl.when(in_group)(functools.partial(emit, d, True))
        else:
            pl.when(in_group & (j % rope_period == 0))(functools.partial(emit, d, True))
            pl.when(in_group & (j % rope_period != 0))(functools.partial(emit, d, False))


def _span_proj(x, g, sh, sc, w, c, s, *, out_scale, rope_period, tn=512):
    S, D = x.shape
    N = w.shape[1]
    tm = SPAN
    vec = pl.BlockSpec((1, D), lambda i, j: (0, 0))
    tab = pl.BlockSpec((tm, LANES), lambda i, j: (i, 0))
    return pl.pallas_call(
        functools.partial(_span_proj_kernel, out_scale=out_scale,
                          tiles_per_group=N // tn // len(DIL_GROUPS), rope_period=rope_period),
        out_shape=jax.ShapeDtypeStruct((S, N), BF16),
        grid=(S // tm, N // tn),
        in_specs=[pl.BlockSpec((tm, D), lambda i, j: (i, 0), pipeline_mode=pl.Buffered(1)), vec, vec, vec,
                  pl.BlockSpec((D, tn), lambda i, j: (0, j)), tab, tab],
        out_specs=pl.BlockSpec((tm, tn), lambda i, j: (i, j)),
        scratch_shapes=[pltpu.VMEM((tm, D), BF16), pltpu.VMEM((tn // LANES, tm, LANES), F32)],
        compiler_params=_params(("parallel", "arbitrary")),
        name="span_proj",
    )(x, g, sh, sc, w, c, s)


def _ffn_in_kernel(x_ref, g_ref, sh_ref, sc_ref, wg_ref, wu_ref, o_ref, h_ref):
    @pl.when(pl.program_id(1) == 0)
    def _():
        _norm_modulate_into(x_ref, g_ref, sh_ref, sc_ref, h_ref)

    h = h_ref[...]
    gate = jnp.dot(h, wg_ref[...], preferred_element_type=F32)
    up = jnp.dot(h, wu_ref[...], preferred_element_type=F32)
    o_ref[...] = (_silu(gate) * up).astype(o_ref.dtype)


def _ffn_in_proj(x, g, sh, sc, w, tm=1024, tn=512):
    S, D = x.shape
    F = w.shape[1] // 2
    nj = F // tn
    vec = pl.BlockSpec((1, D), lambda i, j: (0, 0))
    return pl.pallas_call(
        _ffn_in_kernel,
        out_shape=jax.ShapeDtypeStruct((S, F), BF16),
        grid=(S // tm, nj),
        in_specs=[pl.BlockSpec((tm, D), lambda i, j: (i, 0)), vec, vec, vec,
                  pl.BlockSpec((D, tn), lambda i, j: (0, j)),
                  pl.BlockSpec((D, tn), lambda i, j: (0, j + nj))],
        out_specs=pl.BlockSpec((tm, tn), lambda i, j: (i, j)),
        scratch_shapes=[pltpu.VMEM((tm, D), BF16)],
        compiler_params=_params(("parallel", "arbitrary")),
        name="ffn_in_proj",
    )(x, g, sh, sc, w, w)


def _out_proj_kernel(a_ref, w_ref, r_ref, gt_ref, o_ref):
    acc = jnp.dot(a_ref[...], w_ref[...], preferred_element_type=F32)
    o_ref[...] = r_ref[...] + gt_ref[...] * acc


def _out_proj(a, w, resid, gate, tm=1024, tn=512):
    S, K = a.shape
    N = w.shape[1]
    return pl.pallas_call(
        _out_proj_kernel,
        out_shape=jax.ShapeDtypeStruct((S, N), F32),
        grid=(S // tm, N // tn),
        in_specs=[pl.BlockSpec((tm, K), lambda i, j: (i, 0)),
                  pl.BlockSpec((K, tn), lambda i, j: (0, j)),
                  pl.BlockSpec((tm, tn), lambda i, j: (i, j)),
                  pl.BlockSpec((1, tn), lambda i, j: (0, j))],
        out_specs=pl.BlockSpec((tm, tn), lambda i, j: (i, j)),
        compiler_params=_params(("parallel", "parallel")),
        name="out_proj",
    )(a, w, resid, gate)


def _retention_kernel(lg_ref, q_ref, k_ref, v_ref, g_ref, o_ref, r_ref):
    h = pl.program_id(0)

    @pl.when(pl.program_id(1) == 0)
    def _():
        r_ref[...] = jnp.zeros_like(r_ref)

    C = RET_CHUNK
    lg = lg_ref[h]
    ri = lax.broadcasted_iota(jnp.int32, (C, C), 0)
    ci = lax.broadcasted_iota(jnp.int32, (C, C), 1)
    diff = (ri - ci).astype(F32)
    dmask = jnp.where(diff >= 0, jnp.exp(lg * jnp.maximum(diff, 0.0)), 0.0)
    idx = lax.broadcasted_iota(jnp.int32, (C, 1), 0).astype(F32)
    q_decay = jnp.exp(lg * (idx + 1.0))
    k_decay = jnp.exp(lg * (C - 1.0 - idx))
    chunk_decay = jnp.exp(jnp.full((1, 1), lg * C, F32))

    def body(c, carry):
        rows = pl.ds(pl.multiple_of(c * C, C), C)
        qc = q_ref[rows, :]
        kc = k_ref[rows, :]
        vc = v_ref[rows, :]
        state = r_ref[...]
        scores = lax.dot_general(qc, kc, (((1,), (1,)), ((), ())), preferred_element_type=F32) * dmask
        o = (jnp.dot(scores.astype(BF16), vc, preferred_element_type=F32)
             + jnp.dot(qc, state.astype(BF16), preferred_element_type=F32) * q_decay)
        kd = (kc.astype(F32) * k_decay).astype(BF16)
        r_ref[...] = state * chunk_decay + lax.dot_general(
            kd, vc, (((0,), (0,)), ((), ())), preferred_element_type=F32)
        y = o * lax.rsqrt(jnp.mean(o * o, axis=-1, keepdims=True) + EPS)
        o_ref[rows, :] = (g_ref[rows, :].astype(F32) * y).astype(o_ref.dtype)
        return carry

    lax.fori_loop(0, q_ref.shape[0] // C, body, 0, unroll=2)


def _retention(qkvg, log_g, tb=1024):
    S = qkvg.shape[0]
    kb = RET_HEADS * RET_DK // RET_DK
    vb = 2 * RET_HEADS * RET_DK // RET_DV
    gb = vb + RET_HEADS
    return pl.pallas_call(
        _retention_kernel,
        out_shape=jax.ShapeDtypeStruct((S, RET_HEADS * RET_DV), BF16),
        grid=(RET_HEADS, S // tb),
        in_specs=[pl.BlockSpec(memory_space=pltpu.SMEM),
                  pl.BlockSpec((tb, RET_DK), lambda h, t: (t, h)),
                  pl.BlockSpec((tb, RET_DK), lambda h, t: (t, kb + h)),
                  pl.BlockSpec((tb, RET_DV), lambda h, t: (t, vb + h)),
                  pl.BlockSpec((tb, RET_DV), lambda h, t: (t, gb + h))],
        out_specs=pl.BlockSpec((tb, RET_DV), lambda h, t: (t, h)),
        scratch_shapes=[pltpu.VMEM((RET_DK, RET_DV), F32)],
        compiler_params=_params(("parallel", "arbitrary")),
        name="retention",
    )(log_g, qkvg, qkvg, qkvg, qkvg)


UNIT_UNROLL = 4


def _attn_unit(q_ref, row, k_own, v_own, k_prev, v_prev, bias):
    rep = Q_HEADS // KV_HEADS
    B = ATTN_BLOCK
    q4 = jnp.concatenate([q_ref[pl.ds(row, B), r * HEAD_DIM:(r + 1) * HEAD_DIM] for r in range(rep)], axis=0)
    k_cat = jnp.concatenate([k_prev, k_own], axis=0)
    v_cat = jnp.concatenate([v_prev, v_own], axis=0)
    v_ext = jnp.concatenate([v_cat, jnp.ones_like(v_cat)], axis=1)
    s = lax.dot_general(q4, k_cat, (((1,), (1,)), ((), ())), preferred_element_type=F32) + bias
    m = jnp.max(s, axis=-1, keepdims=True)
    p = jnp.exp2(s - m).astype(BF16)
    ov = jnp.dot(p, v_ext, preferred_element_type=F32)
    l = ov[:, HEAD_DIM:]
    return ov[:, :HEAD_DIM] / l, m + jnp.log2(l)


def _dilated_attn_kernel(q0_ref, q1_ref, q2_ref, k0_ref, v0_ref, k1_ref, v1_ref, k2_ref, v2_ref,
                         hk0_ref, hv0_ref, hk1_ref, hv1_ref, hk2_ref, hv2_ref,
                         o_ref, onat_ref, lnat_ref, bias_ref):
    i = pl.program_id(0)
    B = ATTN_BLOCK
    rep = Q_HEADS // KV_HEADS
    n_units = SPAN // B
    d1, d2 = DIL_GROUPS[1][1], DIL_GROUPS[2][1]

    @pl.when((i == 0) & (pl.program_id(1) == 0))
    def _():
        ri = lax.broadcasted_iota(jnp.int32, (rep * B, 2 * B), 0) % B
        ci = lax.broadcasted_iota(jnp.int32, (rep * B, 2 * B), 1)
        own = (ci >= B) & (ci - B <= ri)
        prev = (ci < B) & (ci >= ri)
        bias_ref[0] = jnp.where(own | prev, 0.0, MASKED)
        bias_ref[1] = jnp.where(own, 0.0, MASKED)

    def scatter(slot, dst, o, lse):
        for hq in range(rep):
            onat_ref[slot, hq, dst, :] = o[hq * B:(hq + 1) * B, :]
            lnat_ref[slot, hq, dst, :] = lse[hq * B:(hq + 1) * B, :]

    def widest_body(r, carry):
        row = pl.multiple_of(r * B, B)
        rows = pl.ds(row, B)
        bias = bias_ref[(i == 0).astype(jnp.int32)]
        o, lse = _attn_unit(q2_ref, row, k2_ref[rows, :], v2_ref[rows, :], hk2_ref[rows, :], hv2_ref[rows, :], bias)
        scatter(1, pl.ds(r, B, stride=d2), o, lse)
        return carry

    lax.fori_loop(0, n_units, widest_body, 0, unroll=UNIT_UNROLL)

    def middle_body(u, carry):
        s = u // d1
        r = u % d1
        row = pl.multiple_of(u * B, B)
        rows = pl.ds(row, B)
        prow = pl.ds(pl.multiple_of(jnp.maximum(u - d1, 0) * B, B), B)
        hrow = pl.ds(pl.multiple_of(r * B, B), B)
        first = s == 0
        k_prev = jnp.where(first, hk1_ref[hrow, :], k1_ref[prow, :])
        v_prev = jnp.where(first, hv1_ref[hrow, :], v1_ref[prow, :])
        bias = bias_ref[(first & (i == 0)).astype(jnp.int32)]
        o, lse = _attn_unit(q1_ref, row, k1_ref[rows, :], v1_ref[rows, :], k_prev, v_prev, bias)
        scatter(0, pl.ds(s * (B * d1) + r, B, stride=d1), o, lse)
        return carry

    lax.fori_loop(0, n_units, middle_body, 0, unroll=UNIT_UNROLL)

    def dense_body(b, carry):
        row = pl.multiple_of(b * B, B)
        rows = pl.ds(row, B)
        prow = pl.ds(pl.multiple_of(jnp.maximum(b - 1, 0) * B, B), B)
        first = b == 0
        k_prev = jnp.where(first, hk0_ref[...], k0_ref[prow, :])
        v_prev = jnp.where(first, hv0_ref[...], v0_ref[prow, :])
        bias = bias_ref[(first & (i == 0)).astype(jnp.int32)]
        o0, l0 = _attn_unit(q0_ref, row, k0_ref[rows, :], v0_ref[rows, :], k_prev, v_prev, bias)
        for hq in range(rep):
            a0 = o0[hq * B:(hq + 1) * B, :]
            e0 = l0[hq * B:(hq + 1) * B, :]
            e1 = lnat_ref[0, hq, rows, :]
            e2 = lnat_ref[1, hq, rows, :]
            mx = jnp.maximum(jnp.maximum(e0, e1), e2)
            w0, w1, w2 = jnp.exp2(e0 - mx), jnp.exp2(e1 - mx), jnp.exp2(e2 - mx)
            merged = (w0 * a0 + w1 * onat_ref[0, hq, rows, :] + w2 * onat_ref[1, hq, rows, :]) / (w0 + w1 + w2)
            o_ref[rows, hq * HEAD_DIM:(hq + 1) * HEAD_DIM] = merged.astype(o_ref.dtype)
        return carry

    lax.fori_loop(0, n_units, dense_body, 0, unroll=UNIT_UNROLL)


def _dilated_attention(q_all, kv_all):
    S = q_all.shape[0]
    rep = Q_HEADS // KV_HEADS
    qw = rep * HEAD_DIM
    per_group = 2 * KV_HEADS
    B = ATTN_BLOCK

    def q_spec(g):
        return pl.BlockSpec((SPAN, qw), lambda i, hd: (i, g * KV_HEADS + hd))

    def kv_spec(g, is_v):
        return pl.BlockSpec((SPAN, HEAD_DIM), lambda i, hd: (i, g * per_group + is_v * KV_HEADS + hd))

    def halo_spec(g, is_v):
        rows = B * DIL_GROUPS[g][1]
        n = SPAN // rows
        return pl.BlockSpec((rows, HEAD_DIM),
                            lambda i, hd: (jnp.maximum(i * n - 1, 0), g * per_group + is_v * KV_HEADS + hd))

    groups = range(len(DIL_GROUPS))
    in_specs = ([q_spec(g) for g in groups]
                + [kv_spec(g, v) for g in groups for v in (0, 1)]
                + [halo_spec(g, v) for g in groups for v in (0, 1)])
    return pl.pallas_call(
        _dilated_attn_kernel,
        out_shape=jax.ShapeDtypeStruct((S, Q_HEADS * HEAD_DIM), BF16),
        grid=(S // SPAN, KV_HEADS),
        in_specs=in_specs,
        out_specs=pl.BlockSpec((SPAN, qw), lambda i, hd: (i, hd)),
        scratch_shapes=[pltpu.VMEM((2, rep, SPAN, LANES), F32),
                        pltpu.VMEM((2, rep, SPAN, LANES), F32),
                        pltpu.VMEM((2, rep * B, 2 * B), F32)],
        compiler_params=_params(("arbitrary", "arbitrary")),
        name="dilated_attn",
    )(*([q_all] * 3 + [kv_all] * 12))


def _final_norm_kernel(x_ref, g_ref, o_ref):
    x = x_ref[...]
    ms = jnp.mean(x * x, axis=-1, keepdims=True)
    o_ref[...] = x * lax.rsqrt(ms + EPS) * g_ref[...]


def _final_norm(x, g, tm=512):
    S, D = x.shape
    return pl.pallas_call(
        _final_norm_kernel,
        out_shape=jax.ShapeDtypeStruct((S, D), F32),
        grid=(S // tm,),
        in_specs=[pl.BlockSpec((tm, D), lambda i: (i, 0)), pl.BlockSpec((1, D), lambda i: (0, 0))],
        out_specs=pl.BlockSpec((tm, D), lambda i: (i, 0)),
        compiler_params=_params(("parallel",)),
        name="final_norm",
    )(x, g)


def kernel(x, c, positions, ada_w, ada_b, norm_g, ffn_w_in, ffn_w_out, ret_w_in, ret_w_out, kv_norm_g, kv_ada_w, kv_ada_b, kv_w, attn_w_q, attn_w_out, final_norm_g):
    B, S, D = x.shape
    assert B == 1 and S % SPAN == 0
    assert all(w // d == ATTN_BLOCK for w, d in DIL_GROUPS) and DIL_GROUPS[0][1] == 1
    xs = x.reshape(S, D)

    mod = _ada_matvec(c, ada_w, ada_b[:, None, :])
    kv_mod = _ada_matvec(c, kv_ada_w[None], kv_ada_b[None, None, :])[0]

    def mod_vec(layer, idx):
        return mod[layer, :, idx * D:(idx + 1) * D]

    inv_freq_ret = 1.0 / (RET_THETA ** jnp.linspace(0.0, 1.0, RET_DK // 2, dtype=F32))
    inv_freq_attn = ROPE_THETA ** (-jnp.arange(0, ROT_DIM, 2, dtype=F32) / ROT_DIM)
    gap = jnp.zeros((ROT_PARTNER - ROT_DIM // 2,), F32)
    inv_freq_attn = jnp.concatenate([inv_freq_attn, gap, inv_freq_attn, gap])
    pos_col = positions.reshape(S, 1).astype(F32)
    cos_r, sin_r, cos_a, sin_a = _rope_tables(pos_col, inv_freq_ret[None, :], inv_freq_attn[None, :])

    qkvg = _ret_in_proj(xs, norm_g[0, 0][None], mod_vec(0, 0), mod_vec(0, 1),
                        ret_w_in[0].astype(BF16), cos_r, sin_r)
    log_g = jnp.log1p(-(2.0 ** (-5.0 - jnp.arange(RET_HEADS, dtype=F32))))
    y = _retention(qkvg, log_g)
    xs = _out_proj(y, ret_w_out[0].astype(BF16), xs, mod_vec(0, 2))
    hid = _ffn_in_proj(xs, norm_g[0, 1][None], mod_vec(0, 3), mod_vec(0, 4), ffn_w_in[0].astype(BF16))
    xs = _out_proj(hid, ffn_w_out[0].astype(BF16), xs, mod_vec(0, 5))

    n_groups = len(DIL_GROUPS)
    kv_cols = KV_HEADS * HEAD_DIM
    kvw = kv_w.reshape(D, n_groups, 2, kv_cols)
    kvw = jnp.stack([_pair_rotary_cols(kvw[:, :, 0].reshape(D, n_groups * kv_cols)).reshape(D, n_groups, kv_cols),
                     kvw[:, :, 1]], axis=2).reshape(D, n_groups * 2 * kv_cols)
    kv_all = _span_proj(xs, kv_norm_g[None], kv_mod[:, :D], kv_mod[:, D:], kvw.astype(BF16),
                        cos_a, sin_a, out_scale=1.0, rope_period=2)

    q_all = _span_proj(xs, norm_g[1, 0][None], mod_vec(1, 0), mod_vec(1, 1),
                       _pair_rotary_cols(attn_w_q[0]).astype(BF16),
                       cos_a, sin_a, out_scale=HEAD_DIM ** -0.5 * LOG2E, rope_period=1)
    attn = _dilated_attention(q_all, kv_all)
    xs = _out_proj(attn, attn_w_out[0].astype(BF16), xs, mod_vec(1, 2))
    hid = _ffn_in_proj(xs, norm_g[1, 1][None], mod_vec(1, 3), mod_vec(1, 4), ffn_w_in[1].astype(BF16))
    xs = _out_proj(hid, ffn_w_out[1].astype(BF16), xs, mod_vec(1, 5))

    return _final_norm(xs, final_norm_g[None]).reshape(B, S, D)
```

```python
import functools

import jax
import jax.numpy as jnp
from jax import lax
from jax.experimental import pallas as pl
from jax.experimental.pallas import tpu as pltpu

F32 = jnp.float32
BF16 = jnp.bfloat16

EPS = 1e-6
RET_HEADS = 8
RET_DK = 256
RET_DV = 512
RET_CHUNK = 128
RET_KERNEL_CHUNK = 256
RET_THETA = 10000.0
DIL_GROUPS = ((128, 1), (512, 4), (2048, 16))
HEAD_DIM = 128
Q_HEADS = 16
KV_HEADS = 4
ROT_DIM = 32
ROPE_THETA = 500000.0
ATTN_BLOCK = 128
N_MOD = 6

LANES = 128
VMEM_LIMIT = 56 * 1024 * 1024
SPAN = ATTN_BLOCK * max(d for _, d in DIL_GROUPS)
ROT_PARTNER = LANES // 2
MASKED = -1e30
LOG2E = 1.4426950408889634


def _params(sem, vmem=VMEM_LIMIT):
    return pltpu.CompilerParams(dimension_semantics=sem, vmem_limit_bytes=vmem)


def _silu(x):
    return x / (1.0 + jnp.exp(-x))


def _matvec_kernel(c_ref, w_ref, b_ref, o_ref):
    c = c_ref[...]
    ca = jnp.broadcast_to(_silu(c), (8, c.shape[1])).astype(BF16)
    acc = jnp.dot(ca, w_ref[...].astype(BF16), preferred_element_type=F32)
    o_ref[...] = acc[0:1, :] + b_ref[...]


def _ada_matvec(c, w, b, tn=1024):
    L, D, N = w.shape
    return pl.pallas_call(
        _matvec_kernel,
        out_shape=jax.ShapeDtypeStruct((L, 1, N), F32),
        grid=(L, N // tn),
        in_specs=[
            pl.BlockSpec((1, D), lambda l, j: (0, 0)),
            pl.BlockSpec((None, D, tn), lambda l, j: (l, 0, j)),
            pl.BlockSpec((None, 1, tn), lambda l, j: (l, 0, j)),
        ],
        out_specs=pl.BlockSpec((None, 1, tn), lambda l, j: (l, 0, j)),
        compiler_params=_params(("parallel", "parallel")),
        name="ada_matvec",
    )(c, w, b)


def _rope_table_kernel(pos_ref, fr_ref, fa_ref, cr_ref, sr_ref, ca_ref, sa_ref):
    pos = pos_ref[...]
    ang_r = pos * fr_ref[...]
    cr_ref[...] = jnp.cos(ang_r)
    sr_ref[...] = jnp.sin(ang_r)
    ang_a = pos * fa_ref[...]
    lane = lax.broadcasted_iota(jnp.int32, ang_a.shape, 1)
    ca_ref[...] = jnp.cos(ang_a)
    sa_ref[...] = jnp.where(lane < ROT_PARTNER, -jnp.sin(ang_a), jnp.sin(ang_a))


def _rope_tables(pos_col, inv_freq_ret, inv_freq_attn, tb=1024):
    S = pos_col.shape[0]
    tab = jax.ShapeDtypeStruct((S, LANES), F32)
    row = pl.BlockSpec((tb, LANES), lambda i: (i, 0))
    vec = pl.BlockSpec((1, LANES), lambda i: (0, 0))
    return pl.pallas_call(
        _rope_table_kernel,
        out_shape=(tab,) * 4,
        grid=(S // tb,),
        in_specs=[pl.BlockSpec((tb, 1), lambda i: (i, 0)), vec, vec],
        out_specs=(row,) * 4,
        compiler_params=_params(("parallel",)),
        name="rope_tables",
    )(pos_col, inv_freq_ret, inv_freq_attn)


def _prep_weight_kernel(w_ref, o_ref, *, pair_period):
    half = ROT_DIM // 2
    shift = ROT_PARTNER - half

    def plain():
        o_ref[...] = w_ref[...].astype(o_ref.dtype)

    def paired():
        for hh in range(w_ref.shape[1] // HEAD_DIM):
            cols = slice(hh * HEAD_DIM, (hh + 1) * HEAD_DIM)
            x = w_ref[:, cols]
            lane = lax.broadcasted_iota(jnp.int32, x.shape, 1)
            from_hi = (lane >= half) & (lane < ROT_DIM)
            from_lo = (lane >= ROT_PARTNER) & (lane < ROT_PARTNER + half)
            y = jnp.where(from_hi, pltpu.roll(x, HEAD_DIM - shift, 1), jnp.where(from_lo, pltpu.roll(x, shift, 1), x))
            o_ref[:, cols] = y.astype(o_ref.dtype)

    if pair_period == 0:
        plain()
    elif pair_period == 1:
        paired()
    else:
        j = pl.program_id(0)
        pl.when(j % pair_period == 0)(paired)
        pl.when(j % pair_period != 0)(plain)


def _prep_weight(w, layer=0, *, bw=512, src_block=None, pair_period=0):
    _, K, N = w.shape
    src = src_block if src_block is not None else (lambda j: j)
    return pl.pallas_call(
        functools.partial(_prep_weight_kernel, pair_period=pair_period),
        out_shape=jax.ShapeDtypeStruct((K, N), BF16),
        grid=(N // bw,),
        in_specs=[pl.BlockSpec((None, K, bw), lambda j: (layer, 0, src(j)))],
        out_specs=pl.BlockSpec((K, bw), lambda j: (0, j)),
        compiler_params=_params(("parallel",)),
        name="prep_weight",
    )(w)


NORM_ROWS = 128


def _norm_modulate_into(x_ref, g_ref, sh_ref, sc_ref, h_ref, row0=0):
    a = g_ref[...] * (1.0 + sc_ref[...])
    b = sh_ref[...]

    def body(c, carry):
        off = pl.multiple_of(c * NORM_ROWS, NORM_ROWS)
        xc = x_ref[pl.ds(off, NORM_ROWS), :]
        ms = jnp.mean(xc * xc, axis=-1, keepdims=True)
        h_ref[pl.ds(pl.multiple_of(row0 + off, NORM_ROWS), NORM_ROWS), :] = (
            xc * lax.rsqrt(ms + EPS) * a + b).astype(BF16)
        return carry

    lax.fori_loop(0, x_ref.shape[0] // NORM_ROWS, body, 0)


RET_HEAD_COLS = 2 * RET_DK + 2 * RET_DV


def _ret_head_src_block(ob):
    per_head = RET_HEAD_COLS // RET_DK
    vblk = RET_DV // RET_DK
    h, p = ob // per_head, ob % per_head
    q_src = h
    k_src = RET_HEADS + h
    v_src = 2 * RET_HEADS + vblk * h + (p - 2)
    g_src = (2 + vblk) * RET_HEADS + vblk * h + (p - 2 - vblk)
    return jnp.where(p == 0, q_src, jnp.where(p == 1, k_src, jnp.where(p < 2 + vblk, v_src, g_src)))


def _ret_in_kernel(x_ref, g_ref, sh_ref, sc_ref, w_ref, cos_ref, sin_ref, o_ref, h_ref):
    @pl.when(pl.program_id(1) == 0)
    def _():
        _norm_modulate_into(x_ref, g_ref, sh_ref, sc_ref, h_ref)

    h = h_ref[...]
    half = RET_DK // 2
    v0, g0 = 2 * RET_DK, 2 * RET_DK + RET_DV

    qk = jnp.dot(h, w_ref[:, :v0], preferred_element_type=F32)
    for lo, scale in ((0, 1.0), (RET_DK, RET_DK ** -0.5)):
        cos = cos_ref[...] * scale
        sin = sin_ref[...] * scale
        a1 = qk[:, lo:lo + half]
        a2 = qk[:, lo + half:lo + RET_DK]
        o_ref[:, lo:lo + half] = (a1 * cos - a2 * sin).astype(o_ref.dtype)
        o_ref[:, lo + half:lo + RET_DK] = (a2 * cos + a1 * sin).astype(o_ref.dtype)

    gate = jnp.dot(h, w_ref[:, g0:], preferred_element_type=F32)
    o_ref[:, g0:] = _silu(gate).astype(o_ref.dtype)
    v = jnp.dot(h, w_ref[:, v0:g0], preferred_element_type=F32)
    o_ref[:, v0:g0] = v.astype(o_ref.dtype)


def _ret_in_proj(x, g, sh, sc, w, cos, sin, tm=1024):
    S, D = x.shape
    N = w.shape[1]
    tn = RET_HEAD_COLS
    vec = pl.BlockSpec((1, D), lambda i, j: (0, 0))
    tab = pl.BlockSpec((tm, LANES), lambda i, j: (i, 0))
    return pl.pallas_call(
        _ret_in_kernel,
        out_shape=jax.ShapeDtypeStruct((S, N), BF16),
        grid=(S // tm, N // tn),
        in_specs=[pl.BlockSpec((tm, D), lambda i, j: (i, 0)), vec, vec, vec,
                  pl.BlockSpec((D, tn), lambda i, j: (0, j)), tab, tab],
        out_specs=pl.BlockSpec((tm, tn), lambda i, j: (i, j)),
        scratch_shapes=[pltpu.VMEM((tm, D), BF16)],
        compiler_params=_params(("parallel", "arbitrary")),
        name="ret_in_proj",
    )(x, g, sh, sc, w, cos, sin)


SPAN_NORM_STEPS = 4
SPAN_SUB = 256
SPAN_TN = 1024


def _span_chunks(d):
    B = ATTN_BLOCK
    out = []
    for s in range(SPAN // (B * d)):
        for r in range(d):
            u = s * d + r
            out.append((pl.ds(s * B * d + r, B, stride=d) if d > 1 else pl.ds(u * B, B), pl.ds(u * B, B)))
    return out


def _span_proj_kernel(x_ref, g_ref, sh_ref, sc_ref, w_ref, c_ref, s_ref, o_ref, h_ref, acc_ref, tab_ref,
                      *, out_scale, steps_per_group, rope_cols):
    j = pl.program_id(1)
    jm = j - SPAN_NORM_STEPS

    @pl.when(jm < 0)
    def _():
        _norm_modulate_into(x_ref, g_ref, sh_ref, sc_ref, h_ref, row0=j * (SPAN // SPAN_NORM_STEPS))

    def run(d):
        chunks = _span_chunks(d)

        @pl.when(jm % steps_per_group == 0)
        def _():
            for src, dst in chunks:
                tab_ref[0, dst, :] = c_ref[src, :] * out_scale
                tab_ref[1, dst, :] = s_ref[src, :] * out_scale

        n_slab = SPAN_SUB // LANES
        for t in range(w_ref.shape[1] // SPAN_SUB):
            col0 = t * SPAN_SUB
            acc = jnp.dot(h_ref[...], w_ref[:, col0:col0 + SPAN_SUB], preferred_element_type=F32)
            buf = t % 2
            for c in range(n_slab):
                acc_ref[buf, c] = acc[:, c * LANES:(c + 1) * LANES]
            rope = col0 < rope_cols
            for src, dst in chunks:
                if rope:
                    ct = tab_ref[0, dst, :]
                    st = tab_ref[1, dst, :]
                for c in range(n_slab):
                    xr = acc_ref[buf, c, src, :]
                    y = xr * ct + pltpu.roll(xr, ROT_PARTNER, 1) * st if rope else xr
                    o_ref[dst, col0 + c * LANES:col0 + (c + 1) * LANES] = y.astype(o_ref.dtype)

    for gi, (_, d) in enumerate(DIL_GROUPS):
        pl.when((jm >= 0) & (jm // steps_per_group == gi))(functools.partial(run, d))


def _span_proj(x, g, sh, sc, w, c, s, *, out_scale, rope_cols):
    S, D = x.shape
    N = w.shape[1]
    tm, tn, ns = SPAN, SPAN_TN, SPAN_NORM_STEPS
    n_mm = N // tn
    vec = pl.BlockSpec((1, D), lambda i, j: (0, 0))
    tab = pl.BlockSpec((tm, LANES), lambda i, j: (i, 0))
    mm_step = lambda j: jnp.maximum(j - ns, 0)
    return pl.pallas_call(
        functools.partial(_span_proj_kernel, out_scale=out_scale,
                          steps_per_group=n_mm // len(DIL_GROUPS), rope_cols=rope_cols),
        out_shape=jax.ShapeDtypeStruct((S, N), BF16),
        grid=(S // tm, ns + n_mm),
        in_specs=[pl.BlockSpec((tm // ns, D), lambda i, j: (i * ns + jnp.minimum(j, ns - 1), 0)), vec, vec, vec,
                  pl.BlockSpec((D, tn), lambda i, j: (0, mm_step(j))), tab, tab],
        out_specs=pl.BlockSpec((tm, tn), lambda i, j: (i, mm_step(j))),
        scratch_shapes=[pltpu.VMEM((tm, D), BF16),
                        pltpu.VMEM((2, SPAN_SUB // LANES, tm, LANES), F32),
                        pltpu.VMEM((2, tm, LANES), F32)],
        compiler_params=_params(("parallel", "arbitrary")),
        name="span_proj",
    )(x, g, sh, sc, w, c, s)


def _ffn_in_kernel(x_ref, g_ref, sh_ref, sc_ref, wg_ref, wu_ref, o_ref, h_ref):
    @pl.when(pl.program_id(1) == 0)
    def _():
        _norm_modulate_into(x_ref, g_ref, sh_ref, sc_ref, h_ref)

    h = h_ref[...]
    gate = jnp.dot(h, wg_ref[...], preferred_element_type=F32)
    up = jnp.dot(h, wu_ref[...], preferred_element_type=F32)
    o_ref[...] = (_silu(gate) * up).astype(o_ref.dtype)


def _ffn_in_proj(x, g, sh, sc, w, tm=1024, tn=512):
    S, D = x.shape
    F = w.shape[1] // 2
    nj = F // tn
    vec = pl.BlockSpec((1, D), lambda i, j: (0, 0))
    return pl.pallas_call(
        _ffn_in_kernel,
        out_shape=jax.ShapeDtypeStruct((S, F), BF16),
        grid=(S // tm, nj),
        in_specs=[pl.BlockSpec((tm, D), lambda i, j: (i, 0)), vec, vec, vec,
                  pl.BlockSpec((D, tn), lambda i, j: (0, j)),
                  pl.BlockSpec((D, tn), lambda i, j: (0, j + nj))],
        out_specs=pl.BlockSpec((tm, tn), lambda i, j: (i, j)),
        scratch_shapes=[pltpu.VMEM((tm, D), BF16)],
        compiler_params=_params(("parallel", "arbitrary")),
        name="ffn_in_proj",
    )(x, g, sh, sc, w, w)


def _out_proj_kernel(a_ref, w_ref, r_ref, gt_ref, o_ref):
    acc = jnp.dot(a_ref[...], w_ref[...], preferred_element_type=F32)
    o_ref[...] = r_ref[...] + gt_ref[...] * acc


def _out_proj(a, w, resid, gate, tm=1024, tn=512):
    S, K = a.shape
    N = w.shape[1]
    return pl.pallas_call(
        _out_proj_kernel,
        out_shape=jax.ShapeDtypeStruct((S, N), F32),
        grid=(S // tm, N // tn),
        in_specs=[pl.BlockSpec((tm, K), lambda i, j: (i, 0)),
                  pl.BlockSpec((K, tn), lambda i, j: (0, j)),
                  pl.BlockSpec((tm, tn), lambda i, j: (i, j)),
                  pl.BlockSpec((1, tn), lambda i, j: (0, j))],
        out_specs=pl.BlockSpec((tm, tn), lambda i, j: (i, j)),
        compiler_params=_params(("parallel", "parallel")),
        name="out_proj",
    )(a, w, resid, gate)


def _retention_kernel(lg_ref, x_ref, o_ref, r_ref):
    h = pl.program_id(0)

    @pl.when(pl.program_id(1) == 0)
    def _():
        r_ref[...] = jnp.zeros_like(r_ref)

    C = RET_KERNEL_CHUNK
    k0, v0, g0 = RET_DK, 2 * RET_DK, 2 * RET_DK + RET_DV
    lg = lg_ref[h]
    ri = lax.broadcasted_iota(jnp.int32, (C, C), 0)
    ci = lax.broadcasted_iota(jnp.int32, (C, C), 1)
    diff = (ri - ci).astype(F32)
    dmask = jnp.where(diff >= 0, jnp.exp(lg * jnp.maximum(diff, 0.0)), 0.0)
    idx = lax.broadcasted_iota(jnp.int32, (C, 1), 0).astype(F32)
    q_decay = jnp.exp(lg * (idx + 1.0))
    k_decay = jnp.exp(lg * (C - 1.0 - idx))
    chunk_decay = jnp.exp(jnp.full((1, 1), lg * C, F32))

    def body(c, carry):
        rows = pl.ds(pl.multiple_of(c * C, C), C)
        qc = x_ref[rows, :k0]
        kc = x_ref[rows, k0:v0]
        vc = x_ref[rows, v0:g0]
        state = r_ref[...]
        scores = lax.dot_general(qc, kc, (((1,), (1,)), ((), ())), preferred_element_type=F32) * dmask
        o = (jnp.dot(scores.astype(BF16), vc, preferred_element_type=F32)
             + jnp.dot(qc, state.astype(BF16), preferred_element_type=F32) * q_decay)
        kd = (kc.astype(F32) * k_decay).astype(BF16)
        r_ref[...] = state * chunk_decay + lax.dot_general(
            kd, vc, (((0,), (0,)), ((), ())), preferred_element_type=F32)
        y = o * lax.rsqrt(jnp.mean(o * o, axis=-1, keepdims=True) + EPS)
        o_ref[rows, :] = (x_ref[rows, g0:].astype(F32) * y).astype(o_ref.dtype)
        return carry

    lax.fori_loop(0, x_ref.shape[0] // C, body, 0, unroll=2)


def _retention(qkvg, log_g, tb=1024):
    S = qkvg.shape[0]
    return pl.pallas_call(
        _retention_kernel,
        out_shape=jax.ShapeDtypeStruct((S, RET_HEADS * RET_DV), BF16),
        grid=(RET_HEADS, S // tb),
        in_specs=[pl.BlockSpec(memory_space=pltpu.SMEM),
                  pl.BlockSpec((tb, RET_HEAD_COLS), lambda h, t: (t, h))],
        out_specs=pl.BlockSpec((tb, RET_DV), lambda h, t: (t, h)),
        scratch_shapes=[pltpu.VMEM((RET_DK, RET_DV), F32)],
        compiler_params=_params(("parallel", "arbitrary")),
        name="retention",
    )(log_g, qkvg)


UNIT_UNROLL = 4


def _attn_unit(q_ref, row, k_own, v_own, k_prev, v_prev, bias):
    rep = Q_HEADS // KV_HEADS
    B = ATTN_BLOCK
    q4 = jnp.concatenate([q_ref[pl.ds(row, B), r * HEAD_DIM:(r + 1) * HEAD_DIM] for r in range(rep)], axis=0)
    k_cat = jnp.concatenate([k_prev, k_own], axis=0)
    v_cat = jnp.concatenate([v_prev, v_own], axis=0)
    v_ext = jnp.concatenate([v_cat, jnp.ones_like(v_cat)], axis=1)
    s = lax.dot_general(q4, k_cat, (((1,), (1,)), ((), ())), preferred_element_type=F32) + bias
    m = jnp.max(s, axis=-1, keepdims=True)
    p = jnp.exp2(s - m).astype(BF16)
    ov = jnp.dot(p, v_ext, preferred_element_type=F32)
    l = ov[:, HEAD_DIM:]
    return ov[:, :HEAD_DIM] / l, m + jnp.log2(l)


def _dilated_attn_kernel(q0_ref, q1_ref, q2_ref, k0_ref, v0_ref, k1_ref, v1_ref, k2_ref, v2_ref,
                         hk0_ref, hv0_ref, hk1_ref, hv1_ref, hk2_ref, hv2_ref,
                         o_ref, onat_ref, lnat_ref, bias_ref):
    i = pl.program_id(0)
    B = ATTN_BLOCK
    rep = Q_HEADS // KV_HEADS
    n_units = SPAN // B
    d1, d2 = DIL_GROUPS[1][1], DIL_GROUPS[2][1]

    @pl.when((i == 0) & (pl.program_id(1) == 0))
    def _():
        ri = lax.broadcasted_iota(jnp.int32, (rep * B, 2 * B), 0) % B
        ci = lax.broadcasted_iota(jnp.int32, (rep * B, 2 * B), 1)
        own = (ci >= B) & (ci - B <= ri)
        prev = (ci < B) & (ci >= ri)
        bias_ref[0] = jnp.where(own | prev, 0.0, MASKED)
        bias_ref[1] = jnp.where(own, 0.0, MASKED)

    def scatter(slot, dst, o, lse):
        for hq in range(rep):
            onat_ref[slot, hq, dst, :] = o[hq * B:(hq + 1) * B, :]
            lnat_ref[slot, hq, dst, :] = lse[hq * B:(hq + 1) * B, :]

    def widest_body(r, carry):
        row = pl.multiple_of(r * B, B)
        rows = pl.ds(row, B)
        bias = bias_ref[(i == 0).astype(jnp.int32)]
        o, lse = _attn_unit(q2_ref, row, k2_ref[rows, :], v2_ref[rows, :], hk2_ref[rows, :], hv2_ref[rows, :], bias)
        scatter(1, pl.ds(r, B, stride=d2), o, lse)
        return carry

    lax.fori_loop(0, n_units, widest_body, 0, unroll=UNIT_UNROLL)

    def middle_body(u, carry):
        s = u // d1
        r = u % d1
        row = pl.multiple_of(u * B, B)
        rows = pl.ds(row, B)
        prow = pl.ds(pl.multiple_of(jnp.maximum(u - d1, 0) * B, B), B)
        hrow = pl.ds(pl.multiple_of(r * B, B), B)
        first = s == 0
        k_prev = jnp.where(first, hk1_ref[hrow, :], k1_ref[prow, :])
        v_prev = jnp.where(first, hv1_ref[hrow, :], v1_ref[prow, :])
        bias = bias_ref[(first & (i == 0)).astype(jnp.int32)]
        o, lse = _attn_unit(q1_ref, row, k1_ref[rows, :], v1_ref[rows, :], k_prev, v_prev, bias)
        scatter(0, pl.ds(s * (B * d1) + r, B, stride=d1), o, lse)
        return carry

    lax.fori_loop(0, n_units, middle_body, 0, unroll=UNIT_UNROLL)

    def dense_body(b, carry):
        row = pl.multiple_of(b * B, B)
        rows = pl.ds(row, B)
        prow = pl.ds(pl.multiple_of(jnp.maximum(b - 1, 0) * B, B), B)
        first = b == 0
        k_prev = jnp.where(first, hk0_ref[...], k0_ref[prow, :])
        v_prev = jnp.where(first, hv0_ref[...], v0_ref[prow, :])
        bias = bias_ref[(first & (i == 0)).astype(jnp.int32)]
        o0, l0 = _attn_unit(q0_ref, row, k0_ref[rows, :], v0_ref[rows, :], k_prev, v_prev, bias)
        for hq in range(rep):
            a0 = o0[hq * B:(hq + 1) * B, :]
            e0 = l0[hq * B:(hq + 1) * B, :]
            e1 = lnat_ref[0, hq, rows, :]
            e2 = lnat_ref[1, hq, rows, :]
            mx = jnp.maximum(jnp.maximum(e0, e1), e2)
            w0, w1, w2 = jnp.exp2(e0 - mx), jnp.exp2(e1 - mx), jnp.exp2(e2 - mx)
            merged = (w0 * a0 + w1 * onat_ref[0, hq, rows, :] + w2 * onat_ref[1, hq, rows, :]) / (w0 + w1 + w2)
            o_ref[rows, hq * HEAD_DIM:(hq + 1) * HEAD_DIM] = merged.astype(o_ref.dtype)
        return carry

    lax.fori_loop(0, n_units, dense_body, 0, unroll=UNIT_UNROLL)


def _dilated_attention(q_all, kv_all):
    S = q_all.shape[0]
    rep = Q_HEADS // KV_HEADS
    qw = rep * HEAD_DIM
    per_group = 2 * KV_HEADS
    B = ATTN_BLOCK

    def q_spec(g):
        return pl.BlockSpec((SPAN, qw), lambda i, hd: (i, g * KV_HEADS + hd))

    def kv_spec(g, is_v):
        return pl.BlockSpec((SPAN, HEAD_DIM), lambda i, hd: (i, g * per_group + is_v * KV_HEADS + hd))

    def halo_spec(g, is_v):
        rows = B * DIL_GROUPS[g][1]
        n = SPAN // rows
        return pl.BlockSpec((rows, HEAD_DIM),
                            lambda i, hd: (jnp.maximum(i * n - 1, 0), g * per_group + is_v * KV_HEADS + hd))

    groups = range(len(DIL_GROUPS))
    in_specs = ([q_spec(g) for g in groups]
                + [kv_spec(g, v) for g in groups for v in (0, 1)]
                + [halo_spec(g, v) for g in groups for v in (0, 1)])
    return pl.pallas_call(
        _dilated_attn_kernel,
        out_shape=jax.ShapeDtypeStruct((S, Q_HEADS * HEAD_DIM), BF16),
        grid=(S // SPAN, KV_HEADS),
        in_specs=in_specs,
        out_specs=pl.BlockSpec((SPAN, qw), lambda i, hd: (i, hd)),
        scratch_shapes=[pltpu.VMEM((2, rep, SPAN, LANES), F32),
                        pltpu.VMEM((2, rep, SPAN, LANES), F32),
                        pltpu.VMEM((2, rep * B, 2 * B), F32)],
        compiler_params=_params(("arbitrary", "arbitrary")),
        name="dilated_attn",
    )(*([q_all] * 3 + [kv_all] * 12))


def _final_norm_kernel(x_ref, g_ref, o_ref):
    x = x_ref[...]
    ms = jnp.mean(x * x, axis=-1, keepdims=True)
    o_ref[...] = x * lax.rsqrt(ms + EPS) * g_ref[...]


def _final_norm(x, g, tm=512):
    S, D = x.shape
    return pl.pallas_call(
        _final_norm_kernel,
        out_shape=jax.ShapeDtypeStruct((S, D), F32),
        grid=(S // tm,),
        in_specs=[pl.BlockSpec((tm, D), lambda i: (i, 0)), pl.BlockSpec((1, D), lambda i: (0, 0))],
        out_specs=pl.BlockSpec((tm, D), lambda i: (i, 0)),
        compiler_params=_params(("parallel",)),
        name="final_norm",
    )(x, g)


def kernel(x, c, positions, ada_w, ada_b, norm_g, ffn_w_in, ffn_w_out, ret_w_in, ret_w_out, kv_norm_g, kv_ada_w, kv_ada_b, kv_w, attn_w_q, attn_w_out, final_norm_g):
    B, S, D = x.shape
    assert B == 1 and S % SPAN == 0
    assert all(w // d == ATTN_BLOCK for w, d in DIL_GROUPS) and DIL_GROUPS[0][1] == 1
    xs = x.reshape(S, D)

    mod = _ada_matvec(c, ada_w, ada_b[:, None, :])
    kv_mod = _ada_matvec(c, kv_ada_w[None], kv_ada_b[None, None, :])[0]

    def mod_vec(layer, idx):
        return mod[layer, :, idx * D:(idx + 1) * D]

    inv_freq_ret = 1.0 / (RET_THETA ** jnp.linspace(0.0, 1.0, RET_DK // 2, dtype=F32))
    inv_freq_attn = ROPE_THETA ** (-jnp.arange(0, ROT_DIM, 2, dtype=F32) / ROT_DIM)
    gap = jnp.zeros((ROT_PARTNER - ROT_DIM // 2,), F32)
    inv_freq_attn = jnp.concatenate([inv_freq_attn, gap, inv_freq_attn, gap])
    pos_col = positions.reshape(S, 1).astype(F32)
    cos_r, sin_r, cos_a, sin_a = _rope_tables(pos_col, inv_freq_ret[None, :], inv_freq_attn[None, :])

    qkvg = _ret_in_proj(xs, norm_g[0, 0][None], mod_vec(0, 0), mod_vec(0, 1),
                        _prep_weight(ret_w_in, bw=RET_DK, src_block=_ret_head_src_block), cos_r, sin_r)
    log_g = jnp.log1p(-(2.0 ** (-5.0 - jnp.arange(RET_HEADS, dtype=F32))))
    y = _retention(qkvg, log_g)
    xs = _out_proj(y, _prep_weight(ret_w_out), xs, mod_vec(0, 2))
    hid = _ffn_in_proj(xs, norm_g[0, 1][None], mod_vec(0, 3), mod_vec(0, 4), _prep_weight(ffn_w_in, 0))
    xs = _out_proj(hid, _prep_weight(ffn_w_out, 0), xs, mod_vec(0, 5))

    kv_cols = KV_HEADS * HEAD_DIM
    kv_all = _span_proj(xs, kv_norm_g[None], kv_mod[:, :D], kv_mod[:, D:],
                        _prep_weight(kv_w[None], bw=kv_cols, pair_period=2),
                        cos_a, sin_a, out_scale=1.0, rope_cols=kv_cols)

    q_all = _span_proj(xs, norm_g[1, 0][None], mod_vec(1, 0), mod_vec(1, 1),
                       _prep_weight(attn_w_q, pair_period=1),
                       cos_a, sin_a, out_scale=HEAD_DIM ** -0.5 * LOG2E, rope_cols=SPAN_TN)
    attn = _dilated_attention(q_all, kv_all)
    xs = _out_proj(attn, _prep_weight(attn_w_out), xs, mod_vec(1, 2))
    hid = _ffn_in_proj(xs, norm_g[1, 1][None], mod_vec(1, 3), mod_vec(1, 4), _prep_weight(ffn_w_in, 1))
    xs = _out_proj(hid, _prep_weight(ffn_w_out, 1), xs, mod_vec(1, 5))

    return _final_norm(xs, final_norm_g[None]).reshape(B, S, D)
```

```python
import functools

import jax
import jax.numpy as jnp
from jax import lax
from jax.experimental import pallas as pl
from jax.experimental.pallas import tpu as pltpu

F32 = jnp.float32
BF16 = jnp.bfloat16

EPS = 1e-6
RET_HEADS = 8
RET_DK = 256
RET_DV = 512
RET_CHUNK = 128
RET_KERNEL_CHUNK = 256
RET_HEADS_PER_STEP = 2
RET_THETA = 10000.0
DIL_GROUPS = ((128, 1), (512, 4), (2048, 16))
HEAD_DIM = 128
Q_HEADS = 16
KV_HEADS = 4
ROT_DIM = 32
ROPE_THETA = 500000.0
ATTN_BLOCK = 128
N_MOD = 6

LANES = 128
VMEM_LIMIT = 56 * 1024 * 1024
SPAN = ATTN_BLOCK * max(d for _, d in DIL_GROUPS)
ROT_PARTNER = LANES // 2
MASKED = -1e30
LOG2E = 1.4426950408889634


def _params(sem, vmem=VMEM_LIMIT):
    return pltpu.CompilerParams(dimension_semantics=sem, vmem_limit_bytes=vmem)


def _silu(x):
    return x / (1.0 + jnp.exp(-x))


def _matvec_kernel(c_ref, w_ref, b_ref, o_ref):
    c = c_ref[...]
    ca = jnp.broadcast_to(_silu(c), (8, c.shape[1])).astype(BF16)
    acc = jnp.dot(ca, w_ref[...].astype(BF16), preferred_element_type=F32)
    o_ref[...] = acc[0:1, :] + b_ref[...]


def _ada_matvec(c, w, b, tn=1024):
    L, D, N = w.shape
    return pl.pallas_call(
        _matvec_kernel,
        out_shape=jax.ShapeDtypeStruct((L, 1, N), F32),
        grid=(L, N // tn),
        in_specs=[
            pl.BlockSpec((1, D), lambda l, j: (0, 0)),
            pl.BlockSpec((None, D, tn), lambda l, j: (l, 0, j)),
            pl.BlockSpec((None, 1, tn), lambda l, j: (l, 0, j)),
        ],
        out_specs=pl.BlockSpec((None, 1, tn), lambda l, j: (l, 0, j)),
        compiler_params=_params(("parallel", "parallel")),
        name="ada_matvec",
    )(c, w, b)


def _rope_table_kernel(pos_ref, fr_ref, fa_ref, cr_ref, sr_ref, ca_ref, sa_ref):
    pos = pos_ref[...]
    ang_r = pos * fr_ref[...]
    cr_ref[...] = jnp.cos(ang_r)
    sr_ref[...] = jnp.sin(ang_r)
    ang_a = pos * fa_ref[...]
    lane = lax.broadcasted_iota(jnp.int32, ang_a.shape, 1)
    ca_ref[...] = jnp.cos(ang_a)
    sa_ref[...] = jnp.where(lane < ROT_PARTNER, -jnp.sin(ang_a), jnp.sin(ang_a))


def _rope_tables(pos_col, inv_freq_ret, inv_freq_attn, tb=1024):
    S = pos_col.shape[0]
    tab = jax.ShapeDtypeStruct((S, LANES), F32)
    row = pl.BlockSpec((tb, LANES), lambda i: (i, 0))
    vec = pl.BlockSpec((1, LANES), lambda i: (0, 0))
    return pl.pallas_call(
        _rope_table_kernel,
        out_shape=(tab,) * 4,
        grid=(S // tb,),
        in_specs=[pl.BlockSpec((tb, 1), lambda i: (i, 0)), vec, vec],
        out_specs=(row,) * 4,
        compiler_params=_params(("parallel",)),
        name="rope_tables",
    )(pos_col, inv_freq_ret, inv_freq_attn)


PAIR_TILE = KV_HEADS * HEAD_DIM


def _cast_block(w_ref, o_ref, pair_period=0):
    if pair_period == 0:
        o_ref[...] = w_ref[...].astype(o_ref.dtype)
        return
    half = ROT_DIM // 2
    shift = ROT_PARTNER - half
    for ct in range(w_ref.shape[1] // PAIR_TILE):
        if ct % pair_period != 0:
            cols = slice(ct * PAIR_TILE, (ct + 1) * PAIR_TILE)
            o_ref[:, cols] = w_ref[:, cols].astype(o_ref.dtype)
            continue
        for hh in range(PAIR_TILE // HEAD_DIM):
            cols = slice(ct * PAIR_TILE + hh * HEAD_DIM, ct * PAIR_TILE + (hh + 1) * HEAD_DIM)
            x = w_ref[:, cols]
            lane = lax.broadcasted_iota(jnp.int32, x.shape, 1)
            from_hi = (lane >= half) & (lane < ROT_DIM)
            from_lo = (lane >= ROT_PARTNER) & (lane < ROT_PARTNER + half)
            y = jnp.where(from_hi, pltpu.roll(x, HEAD_DIM - shift, 1), jnp.where(from_lo, pltpu.roll(x, shift, 1), x))
            o_ref[:, cols] = y.astype(o_ref.dtype)


def _prep_weight_kernel(w_ref, o_ref):
    _cast_block(w_ref, o_ref)


def _prep_weight(w, layer=0, *, bw=512, src_block=None):
    _, K, N = w.shape
    src = src_block if src_block is not None else (lambda j: j)
    return pl.pallas_call(
        _prep_weight_kernel,
        out_shape=jax.ShapeDtypeStruct((K, N), BF16),
        grid=(N // bw,),
        in_specs=[pl.BlockSpec((None, K, bw), lambda j: (layer, 0, src(j)))],
        out_specs=pl.BlockSpec((K, bw), lambda j: (0, j)),
        compiler_params=_params(("parallel",)),
        name="prep_weight",
    )(w)


class _SideCast:
    def __init__(self, w, layer=0, pair_period=0):
        self.w, self.layer, self.pair_period = w, layer, pair_period

    def plan(self, grid):
        _, K, N = self.w.shape
        n_steps, nj, layer = grid[0] * grid[1], grid[1], self.layer
        bf16_rows = 16
        n_chunks = max(n for n in range(1, n_steps + 1) if K % (n * bf16_rows) == 0)
        rows = K // n_chunks
        chunk = lambda i, j: jnp.minimum(i * nj + j, n_chunks - 1)
        return (pl.BlockSpec((None, rows, N), lambda i, j: (layer, chunk(i, j), 0)),
                pl.BlockSpec((rows, N), lambda i, j: (chunk(i, j), 0)),
                jax.ShapeDtypeStruct((K, N), BF16))


NORM_ROWS = 128


def _norm_modulate_into(x_ref, g_ref, sh_ref, sc_ref, h_ref, row0=0):
    a = g_ref[...] * (1.0 + sc_ref[...])
    b = sh_ref[...]

    def body(c, carry):
        off = pl.multiple_of(c * NORM_ROWS, NORM_ROWS)
        xc = x_ref[pl.ds(off, NORM_ROWS), :]
        ms = jnp.mean(xc * xc, axis=-1, keepdims=True)
        h_ref[pl.ds(pl.multiple_of(row0 + off, NORM_ROWS), NORM_ROWS), :] = (
            xc * lax.rsqrt(ms + EPS) * a + b).astype(BF16)
        return carry

    lax.fori_loop(0, x_ref.shape[0] // NORM_ROWS, body, 0)


RET_HEAD_COLS = 2 * RET_DK + 2 * RET_DV


def _ret_head_src_block(ob):
    per_head = RET_HEAD_COLS // RET_DK
    vblk = RET_DV // RET_DK
    h, p = ob // per_head, ob % per_head
    q_src = h
    k_src = RET_HEADS + h
    v_src = 2 * RET_HEADS + vblk * h + (p - 2)
    g_src = (2 + vblk) * RET_HEADS + vblk * h + (p - 2 - vblk)
    return jnp.where(p == 0, q_src, jnp.where(p == 1, k_src, jnp.where(p < 2 + vblk, v_src, g_src)))


def _ret_in_kernel(x_ref, g_ref, sh_ref, sc_ref, w_ref, cos_ref, sin_ref, sw_ref, o_ref, so_ref, h_ref):
    _cast_block(sw_ref, so_ref)

    @pl.when(pl.program_id(1) == 0)
    def _():
        _norm_modulate_into(x_ref, g_ref, sh_ref, sc_ref, h_ref)

    h = h_ref[...]
    half = RET_DK // 2
    v0, g0 = 2 * RET_DK, 2 * RET_DK + RET_DV

    qk = jnp.dot(h, w_ref[:, :v0], preferred_element_type=F32)
    for lo, scale in ((0, 1.0), (RET_DK, RET_DK ** -0.5)):
        cos = cos_ref[...] * scale
        sin = sin_ref[...] * scale
        a1 = qk[:, lo:lo + half]
        a2 = qk[:, lo + half:lo + RET_DK]
        o_ref[:, lo:lo + half] = (a1 * cos - a2 * sin).astype(o_ref.dtype)
        o_ref[:, lo + half:lo + RET_DK] = (a2 * cos + a1 * sin).astype(o_ref.dtype)

    gate = jnp.dot(h, w_ref[:, g0:], preferred_element_type=F32)
    o_ref[:, g0:] = _silu(gate).astype(o_ref.dtype)
    v = jnp.dot(h, w_ref[:, v0:g0], preferred_element_type=F32)
    o_ref[:, v0:g0] = v.astype(o_ref.dtype)


def _ret_in_proj(x, g, sh, sc, w, cos, sin, side, tm=1024):
    S, D = x.shape
    N = w.shape[1]
    tn = RET_HEAD_COLS
    grid = (S // tm, N // tn)
    vec = pl.BlockSpec((1, D), lambda i, j: (0, 0))
    tab = pl.BlockSpec((tm, LANES), lambda i, j: (i, 0))
    side_in, side_out, side_shape = side.plan(grid)
    return pl.pallas_call(
        _ret_in_kernel,
        out_shape=(jax.ShapeDtypeStruct((S, N), BF16), side_shape),
        grid=grid,
        in_specs=[pl.BlockSpec((tm, D), lambda i, j: (i, 0)), vec, vec, vec,
                  pl.BlockSpec((D, tn), lambda i, j: (0, j)), tab, tab, side_in],
        out_specs=(pl.BlockSpec((tm, tn), lambda i, j: (i, j)), side_out),
        scratch_shapes=[pltpu.VMEM((tm, D), BF16)],
        compiler_params=_params(("arbitrary", "arbitrary")),
        name="ret_in_proj",
    )(x, g, sh, sc, w, cos, sin, side.w)


SPAN_NORM_STEPS = 4
SPAN_SUB = 256
SPAN_TN = 1024


def _span_chunks(d):
    B = ATTN_BLOCK
    out = []
    for s in range(SPAN // (B * d)):
        for r in range(d):
            u = s * d + r
            out.append((pl.ds(s * B * d + r, B, stride=d) if d > 1 else pl.ds(u * B, B), pl.ds(u * B, B)))
    return out


def _span_proj_kernel(x_ref, g_ref, sh_ref, sc_ref, w_ref, c_ref, s_ref, sw_ref, o_ref, so_ref,
                      h_ref, acc_ref, tab_ref, *, out_scale, steps_per_group, rope_cols):
    j = pl.program_id(1)
    jm = j - SPAN_NORM_STEPS
    _cast_block(sw_ref, so_ref)

    @pl.when(jm < 0)
    def _():
        _norm_modulate_into(x_ref, g_ref, sh_ref, sc_ref, h_ref, row0=j * (SPAN // SPAN_NORM_STEPS))

    def run(d):
        chunks = _span_chunks(d)

        @pl.when(jm % steps_per_group == 0)
        def _():
            for src, dst in chunks:
                tab_ref[0, dst, :] = c_ref[src, :] * out_scale
                tab_ref[1, dst, :] = s_ref[src, :] * out_scale

        n_slab = SPAN_SUB // LANES
        for t in range(w_ref.shape[1] // SPAN_SUB):
            col0 = t * SPAN_SUB
            acc = jnp.dot(h_ref[...], w_ref[:, col0:col0 + SPAN_SUB], preferred_element_type=F32)
            buf = t % 2
            for c in range(n_slab):
                acc_ref[buf, c] = acc[:, c * LANES:(c + 1) * LANES]
            rope = col0 < rope_cols
            for src, dst in chunks:
                if rope:
                    ct = tab_ref[0, dst, :]
                    st = tab_ref[1, dst, :]
                for c in range(n_slab):
                    xr = acc_ref[buf, c, src, :]
                    y = xr * ct + pltpu.roll(xr, ROT_PARTNER, 1) * st if rope else xr
                    o_ref[dst, col0 + c * LANES:col0 + (c + 1) * LANES] = y.astype(o_ref.dtype)

    for gi, (_, d) in enumerate(DIL_GROUPS):
        pl.when((jm >= 0) & (jm // steps_per_group == gi))(functools.partial(run, d))


def _span_proj(x, g, sh, sc, w, c, s, side, *, out_scale, rope_cols):
    S, D = x.shape
    N = w.shape[1]
    tm, tn, ns = SPAN, SPAN_TN, SPAN_NORM_STEPS
    n_mm = N // tn
    grid = (S // tm, ns + n_mm)
    vec = pl.BlockSpec((1, D), lambda i, j: (0, 0))
    tab = pl.BlockSpec((tm, LANES), lambda i, j: (i, 0))
    mm_step = lambda j: jnp.maximum(j - ns, 0)
    side_in, side_out, side_shape = side.plan(grid)
    return pl.pallas_call(
        functools.partial(_span_proj_kernel, out_scale=out_scale,
                          steps_per_group=n_mm // len(DIL_GROUPS), rope_cols=rope_cols),
        out_shape=(jax.ShapeDtypeStruct((S, N), BF16), side_shape),
        grid=grid,
        in_specs=[pl.BlockSpec((tm // ns, D), lambda i, j: (i * ns + jnp.minimum(j, ns - 1), 0)), vec, vec, vec,
                  pl.BlockSpec((D, tn), lambda i, j: (0, mm_step(j))), tab, tab, side_in],
        out_specs=(pl.BlockSpec((tm, tn), lambda i, j: (i, mm_step(j))), side_out),
        scratch_shapes=[pltpu.VMEM((tm, D), BF16),
                        pltpu.VMEM((2, SPAN_SUB // LANES, tm, LANES), F32),
                        pltpu.VMEM((2, tm, LANES), F32)],
        compiler_params=_params(("arbitrary", "arbitrary")),
        name="span_proj",
    )(x, g, sh, sc, w, c, s, side.w)


def _with_side(side, grid, in_specs, out_spec, out_shape, operands):
    if side is None:
        return in_specs, out_spec, out_shape, operands
    side_in, side_out, side_shape = side.plan(grid)
    return in_specs + [side_in], (out_spec, side_out), (out_shape, side_shape), operands + (side.w,)


def _ffn_in_kernel(x_ref, g_ref, sh_ref, sc_ref, wg_ref, wu_ref, *rest, side_pair):
    if side_pair is None:
        o_ref, h_ref = rest
    else:
        sw_ref, o_ref, so_ref, h_ref = rest
        _cast_block(sw_ref, so_ref, side_pair)

    @pl.when(pl.program_id(1) == 0)
    def _():
        _norm_modulate_into(x_ref, g_ref, sh_ref, sc_ref, h_ref)

    h = h_ref[...]
    gate = jnp.dot(h, wg_ref[...], preferred_element_type=F32)
    up = jnp.dot(h, wu_ref[...], preferred_element_type=F32)
    o_ref[...] = (_silu(gate) * up).astype(o_ref.dtype)


def _ffn_in_proj(x, g, sh, sc, w, side=None, tm=1024, tn=512):
    S, D = x.shape
    F = w.shape[1] // 2
    nj = F // tn
    grid = (S // tm, nj)
    vec = pl.BlockSpec((1, D), lambda i, j: (0, 0))
    in_specs, out_specs, out_shape, operands = _with_side(
        side, grid,
        [pl.BlockSpec((tm, D), lambda i, j: (i, 0)), vec, vec, vec,
         pl.BlockSpec((D, tn), lambda i, j: (0, j)),
         pl.BlockSpec((D, tn), lambda i, j: (0, j + nj))],
        pl.BlockSpec((tm, tn), lambda i, j: (i, j)),
        jax.ShapeDtypeStruct((S, F), BF16),
        (x, g, sh, sc, w, w))
    return pl.pallas_call(
        functools.partial(_ffn_in_kernel, side_pair=None if side is None else side.pair_period),
        out_shape=out_shape,
        grid=grid,
        in_specs=in_specs,
        out_specs=out_specs,
        scratch_shapes=[pltpu.VMEM((tm, D), BF16)],
        compiler_params=_params(("arbitrary", "arbitrary")),
        name="ffn_in_proj",
    )(*operands)


def _out_proj_kernel(a_ref, w_ref, r_ref, gt_ref, *rest, side_pair):
    if side_pair is None:
        (o_ref,) = rest
    else:
        sw_ref, o_ref, so_ref = rest
        _cast_block(sw_ref, so_ref, side_pair)
    acc = jnp.dot(a_ref[...], w_ref[...], preferred_element_type=F32)
    o_ref[...] = r_ref[...] + gt_ref[...] * acc


def _out_proj(a, w, resid, gate, side=None, tm=1024, tn=512):
    S, K = a.shape
    N = w.shape[1]
    grid = (S // tm, N // tn)
    in_specs, out_specs, out_shape, operands = _with_side(
        side, grid,
        [pl.BlockSpec((tm, K), lambda i, j: (i, 0)),
         pl.BlockSpec((K, tn), lambda i, j: (0, j)),
         pl.BlockSpec((tm, tn), lambda i, j: (i, j)),
         pl.BlockSpec((1, tn), lambda i, j: (0, j))],
        pl.BlockSpec((tm, tn), lambda i, j: (i, j)),
        jax.ShapeDtypeStruct((S, N), F32),
        (a, w, resid, gate))
    return pl.pallas_call(
        functools.partial(_out_proj_kernel, side_pair=None if side is None else side.pair_period),
        out_shape=out_shape,
        grid=grid,
        in_specs=in_specs,
        out_specs=out_specs,
        compiler_params=_params(("arbitrary", "arbitrary")),
        name="out_proj",
    )(*operands)


def _retention_kernel(lg_ref, x_ref, sw_ref, o_ref, so_ref, r_ref):
    _cast_block(sw_ref, so_ref)

    @pl.when(pl.program_id(1) == 0)
    def _():
        r_ref[...] = jnp.zeros_like(r_ref)

    C = RET_KERNEL_CHUNK
    k0, v0, g0 = RET_DK, 2 * RET_DK, 2 * RET_DK + RET_DV
    ri = lax.broadcasted_iota(jnp.int32, (C, C), 0)
    ci = lax.broadcasted_iota(jnp.int32, (C, C), 1)
    diff = (ri - ci).astype(F32)
    idx = lax.broadcasted_iota(jnp.int32, (C, 1), 0).astype(F32)
    decays = []
    for hh in range(RET_HEADS_PER_STEP):
        lg = lg_ref[pl.program_id(0) * RET_HEADS_PER_STEP + hh]
        decays.append((jnp.where(diff >= 0, jnp.exp(lg * jnp.maximum(diff, 0.0)), 0.0),
                       jnp.exp(lg * (idx + 1.0)),
                       jnp.exp(lg * (C - 1.0 - idx)),
                       jnp.exp(jnp.full((1, 1), lg * C, F32))))

    def body(c, carry):
        rows = pl.ds(pl.multiple_of(c * C, C), C)
        for hh, (dmask, q_decay, k_decay, chunk_decay) in enumerate(decays):
            x0 = hh * RET_HEAD_COLS
            qc = x_ref[rows, x0:x0 + k0]
            kc = x_ref[rows, x0 + k0:x0 + v0]
            vc = x_ref[rows, x0 + v0:x0 + g0]
            state = r_ref[hh]
            scores = lax.dot_general(qc, kc, (((1,), (1,)), ((), ())), preferred_element_type=F32) * dmask
            o = (jnp.dot(scores.astype(BF16), vc, preferred_element_type=F32)
                 + jnp.dot(qc, state.astype(BF16), preferred_element_type=F32) * q_decay)
            kd = (kc.astype(F32) * k_decay).astype(BF16)
            r_ref[hh] = state * chunk_decay + lax.dot_general(
                kd, vc, (((0,), (0,)), ((), ())), preferred_element_type=F32)
            y = o * lax.rsqrt(jnp.mean(o * o, axis=-1, keepdims=True) + EPS)
            gate = x_ref[rows, x0 + g0:x0 + RET_HEAD_COLS].astype(F32)
            o_ref[rows, hh * RET_DV:(hh + 1) * RET_DV] = (gate * y).astype(o_ref.dtype)
        return carry

    lax.fori_loop(0, x_ref.shape[0] // C, body, 0, unroll=2)


def _retention(qkvg, log_g, side, tb=1024):
    S = qkvg.shape[0]
    hps = RET_HEADS_PER_STEP
    grid = (RET_HEADS // hps, S // tb)
    side_in, side_out, side_shape = side.plan(grid)
    return pl.pallas_call(
        _retention_kernel,
        out_shape=(jax.ShapeDtypeStruct((S, RET_HEADS * RET_DV), BF16), side_shape),
        grid=grid,
        in_specs=[pl.BlockSpec(memory_space=pltpu.SMEM),
                  pl.BlockSpec((tb, hps * RET_HEAD_COLS), lambda h, t: (t, h)), side_in],
        out_specs=(pl.BlockSpec((tb, hps * RET_DV), lambda h, t: (t, h)), side_out),
        scratch_shapes=[pltpu.VMEM((hps, RET_DK, RET_DV), F32)],
        compiler_params=_params(("arbitrary", "arbitrary")),
        name="retention",
    )(log_g, qkvg, side.w)


UNIT_UNROLL = 4


def _attn_unit(q_ref, row, k_own, v_own, k_prev, v_prev, bias):
    rep = Q_HEADS // KV_HEADS
    B = ATTN_BLOCK
    q4 = jnp.concatenate([q_ref[pl.ds(row, B), r * HEAD_DIM:(r + 1) * HEAD_DIM] for r in range(rep)], axis=0)
    k_cat = jnp.concatenate([k_prev, k_own], axis=0)
    v_cat = jnp.concatenate([v_prev, v_own], axis=0)
    v_ext = jnp.concatenate([v_cat, jnp.ones_like(v_cat)], axis=1)
    s = lax.dot_general(q4, k_cat, (((1,), (1,)), ((), ())), preferred_element_type=F32) + bias
    m = jnp.max(s, axis=-1, keepdims=True)
    p = jnp.exp2(s - m).astype(BF16)
    ov = jnp.dot(p, v_ext, preferred_element_type=F32)
    l = ov[:, HEAD_DIM:]
    return ov[:, :HEAD_DIM] / l, m + jnp.log2(l)


def _dilated_attn_kernel(q0_ref, q1_ref, q2_ref, k0_ref, v0_ref, k1_ref, v1_ref, k2_ref, v2_ref,
                         hk0_ref, hv0_ref, hk1_ref, hv1_ref, hk2_ref, hv2_ref, sw_ref,
                         o_ref, so_ref, onat_ref, lnat_ref, bias_ref):
    _cast_block(sw_ref, so_ref)
    i = pl.program_id(0)
    B = ATTN_BLOCK
    rep = Q_HEADS // KV_HEADS
    n_units = SPAN // B
    d1, d2 = DIL_GROUPS[1][1], DIL_GROUPS[2][1]

    @pl.when((i == 0) & (pl.program_id(1) == 0))
    def _():
        ri = lax.broadcasted_iota(jnp.int32, (rep * B, 2 * B), 0) % B
        ci = lax.broadcasted_iota(jnp.int32, (rep * B, 2 * B), 1)
        own = (ci >= B) & (ci - B <= ri)
        prev = (ci < B) & (ci >= ri)
        bias_ref[0] = jnp.where(own | prev, 0.0, MASKED)
        bias_ref[1] = jnp.where(own, 0.0, MASKED)

    def scatter(slot, dst, o, lse):
        for hq in range(rep):
            onat_ref[slot, hq, dst, :] = o[hq * B:(hq + 1) * B, :]
            lnat_ref[slot, hq, dst, :] = lse[hq * B:(hq + 1) * B, :]

    def widest_body(r, carry):
        row = pl.multiple_of(r * B, B)
        rows = pl.ds(row, B)
        bias = bias_ref[(i == 0).astype(jnp.int32)]
        o, lse = _attn_unit(q2_ref, row, k2_ref[rows, :], v2_ref[rows, :], hk2_ref[rows, :], hv2_ref[rows, :], bias)
        scatter(1, pl.ds(r, B, stride=d2), o, lse)
        return carry

    lax.fori_loop(0, n_units, widest_body, 0, unroll=UNIT_UNROLL)

    def middle_body(u, carry):
        s = u // d1
        r = u % d1
        row = pl.multiple_of(u * B, B)
        rows = pl.ds(row, B)
        prow = pl.ds(pl.multiple_of(jnp.maximum(u - d1, 0) * B, B), B)
        hrow = pl.ds(pl.multiple_of(r * B, B), B)
        first = s == 0
        k_prev = jnp.where(first, hk1_ref[hrow, :], k1_ref[prow, :])
        v_prev = jnp.where(first, hv1_ref[hrow, :], v1_ref[prow, :])
        bias = bias_ref[(first & (i == 0)).astype(jnp.int32)]
        o, lse = _attn_unit(q1_ref, row, k1_ref[rows, :], v1_ref[rows, :], k_prev, v_prev, bias)
        scatter(0, pl.ds(s * (B * d1) + r, B, stride=d1), o, lse)
        return carry

    lax.fori_loop(0, n_units, middle_body, 0, unroll=UNIT_UNROLL)

    def dense_body(b, carry):
        row = pl.multiple_of(b * B, B)
        rows = pl.ds(row, B)
        prow = pl.ds(pl.multiple_of(jnp.maximum(b - 1, 0) * B, B), B)
        first = b == 0
        k_prev = jnp.where(first, hk0_ref[...], k0_ref[prow, :])
        v_prev = jnp.where(first, hv0_ref[...], v0_ref[prow, :])
        bias = bias_ref[(first & (i == 0)).astype(jnp.int32)]
        o0, l0 = _attn_unit(q0_ref, row, k0_ref[rows, :], v0_ref[rows, :], k_prev, v_prev, bias)
        for hq in range(rep):
            a0 = o0[hq * B:(hq + 1) * B, :]
            e0 = l0[hq * B:(hq + 1) * B, :]
            e1 = lnat_ref[0, hq, rows, :]
            e2 = lnat_ref[1, hq, rows, :]
            mx = jnp.maximum(jnp.maximum(e0, e1), e2)
            w0, w1, w2 = jnp.exp2(e0 - mx), jnp.exp2(e1 - mx), jnp.exp2(e2 - mx)
            merged = (w0 * a0 + w1 * onat_ref[0, hq, rows, :] + w2 * onat_ref[1, hq, rows, :]) / (w0 + w1 + w2)
            o_ref[rows, hq * HEAD_DIM:(hq + 1) * HEAD_DIM] = merged.astype(o_ref.dtype)
        return carry

    lax.fori_loop(0, n_units, dense_body, 0, unroll=UNIT_UNROLL)


def _dilated_attention(q_all, kv_all, side):
    S = q_all.shape[0]
    rep = Q_HEADS // KV_HEADS
    qw = rep * HEAD_DIM
    per_group = 2 * KV_HEADS
    B = ATTN_BLOCK

    def q_spec(g):
        return pl.BlockSpec((SPAN, qw), lambda i, hd: (i, g * KV_HEADS + hd))

    def kv_spec(g, is_v):
        return pl.BlockSpec((SPAN, HEAD_DIM), lambda i, hd: (i, g * per_group + is_v * KV_HEADS + hd))

    def halo_spec(g, is_v):
        rows = B * DIL_GROUPS[g][1]
        n = SPAN // rows
        return pl.BlockSpec((rows, HEAD_DIM),
                            lambda i, hd: (jnp.maximum(i * n - 1, 0), g * per_group + is_v * KV_HEADS + hd))

    groups = range(len(DIL_GROUPS))
    in_specs = ([q_spec(g) for g in groups]
                + [kv_spec(g, v) for g in groups for v in (0, 1)]
                + [halo_spec(g, v) for g in groups for v in (0, 1)])
    grid = (S // SPAN, KV_HEADS)
    side_in, side_out, side_shape = side.plan(grid)
    return pl.pallas_call(
        _dilated_attn_kernel,
        out_shape=(jax.ShapeDtypeStruct((S, Q_HEADS * HEAD_DIM), BF16), side_shape),
        grid=grid,
        in_specs=in_specs + [side_in],
        out_specs=(pl.BlockSpec((SPAN, qw), lambda i, hd: (i, hd)), side_out),
        scratch_shapes=[pltpu.VMEM((2, rep, SPAN, LANES), F32),
                        pltpu.VMEM((2, rep, SPAN, LANES), F32),
                        pltpu.VMEM((2, rep * B, 2 * B), F32)],
        compiler_params=_params(("arbitrary", "arbitrary")),
        name="dilated_attn",
    )(*([q_all] * 3 + [kv_all] * 12 + [side.w]))


def _final_norm_kernel(x_ref, g_ref, o_ref):
    x = x_ref[...]
    ms = jnp.mean(x * x, axis=-1, keepdims=True)
    o_ref[...] = x * lax.rsqrt(ms + EPS) * g_ref[...]


def _final_norm(x, g, tm=512):
    S, D = x.shape
    return pl.pallas_call(
        _final_norm_kernel,
        out_shape=jax.ShapeDtypeStruct((S, D), F32),
        grid=(S // tm,),
        in_specs=[pl.BlockSpec((tm, D), lambda i: (i, 0)), pl.BlockSpec((1, D), lambda i: (0, 0))],
        out_specs=pl.BlockSpec((tm, D), lambda i: (i, 0)),
        compiler_params=_params(("parallel",)),
        name="final_norm",
    )(x, g)


def kernel(x, c, positions, ada_w, ada_b, norm_g, ffn_w_in, ffn_w_out, ret_w_in, ret_w_out, kv_norm_g, kv_ada_w, kv_ada_b, kv_w, attn_w_q, attn_w_out, final_norm_g):
    B, S, D = x.shape
    assert B == 1 and S % SPAN == 0
    assert all(w // d == ATTN_BLOCK for w, d in DIL_GROUPS) and DIL_GROUPS[0][1] == 1
    xs = x.reshape(S, D)

    mod = _ada_matvec(c, ada_w, ada_b[:, None, :])
    kv_mod = _ada_matvec(c, kv_ada_w[None], kv_ada_b[None, None, :])[0]

    def mod_vec(layer, idx):
        return mod[layer, :, idx * D:(idx + 1) * D]

    inv_freq_ret = 1.0 / (RET_THETA ** jnp.linspace(0.0, 1.0, RET_DK // 2, dtype=F32))
    inv_freq_attn = ROPE_THETA ** (-jnp.arange(0, ROT_DIM, 2, dtype=F32) / ROT_DIM)
    gap = jnp.zeros((ROT_PARTNER - ROT_DIM // 2,), F32)
    inv_freq_attn = jnp.concatenate([inv_freq_attn, gap, inv_freq_attn, gap])
    pos_col = positions.reshape(S, 1).astype(F32)
    cos_r, sin_r, cos_a, sin_a = _rope_tables(pos_col, inv_freq_ret[None, :], inv_freq_attn[None, :])

    w_ret_in = _prep_weight(ret_w_in, bw=RET_DK, src_block=_ret_head_src_block)
    qkvg, w_ret_out = _ret_in_proj(xs, norm_g[0, 0][None], mod_vec(0, 0), mod_vec(0, 1), w_ret_in, cos_r, sin_r,
                                   _SideCast(ret_w_out))
    log_g = jnp.log1p(-(2.0 ** (-5.0 - jnp.arange(RET_HEADS, dtype=F32))))
    y, w_ffn_in0 = _retention(qkvg, log_g, _SideCast(ffn_w_in, 0))
    xs, w_kv = _out_proj(y, w_ret_out, xs, mod_vec(0, 2), _SideCast(kv_w[None], pair_period=2))
    hid, w_ffn_out0 = _ffn_in_proj(xs, norm_g[0, 1][None], mod_vec(0, 3), mod_vec(0, 4), w_ffn_in0,
                                   _SideCast(ffn_w_out, 0))
    xs, w_q = _out_proj(hid, w_ffn_out0, xs, mod_vec(0, 5), _SideCast(attn_w_q, pair_period=1))

    kv_all, w_attn_out = _span_proj(xs, kv_norm_g[None], kv_mod[:, :D], kv_mod[:, D:], w_kv, cos_a, sin_a,
                                    _SideCast(attn_w_out), out_scale=1.0, rope_cols=PAIR_TILE)

    q_all, w_ffn_in1 = _span_proj(xs, norm_g[1, 0][None], mod_vec(1, 0), mod_vec(1, 1), w_q, cos_a, sin_a,
                                  _SideCast(ffn_w_in, 1), out_scale=HEAD_DIM ** -0.5 * LOG2E, rope_cols=SPAN_TN)
    attn, w_ffn_out1 = _dilated_attention(q_all, kv_all, _SideCast(ffn_w_out, 1))
    xs = _out_proj(attn, w_attn_out, xs, mod_vec(1, 2))
    hid = _ffn_in_proj(xs, norm_g[1, 1][None], mod_vec(1, 3), mod_vec(1, 4), w_ffn_in1)
    xs = _out_proj(hid, w_ffn_out1, xs, mod_vec(1, 5))

    return _final_norm(xs, final_norm_g[None]).reshape(B, S, D)
```

```python
import functools

import jax
import jax.numpy as jnp
from jax import lax
from jax.experimental import pallas as pl
from jax.experimental.pallas import tpu as pltpu

F32 = jnp.float32
BF16 = jnp.bfloat16

EPS = 1e-6
RET_HEADS = 8
RET_DK = 256
RET_DV = 512
RET_CHUNK = 128
RET_KERNEL_CHUNK = 256
RET_HEADS_PER_STEP = 2
RET_THETA = 10000.0
DIL_GROUPS = ((128, 1), (512, 4), (2048, 16))
HEAD_DIM = 128
Q_HEADS = 16
KV_HEADS = 4
ROT_DIM = 32
ROPE_THETA = 500000.0
ATTN_BLOCK = 128
N_MOD = 6

LANES = 128
VMEM_LIMIT = 56 * 1024 * 1024
SPAN = ATTN_BLOCK * max(d for _, d in DIL_GROUPS)
ROT_PARTNER = LANES // 2
MASKED = -1e30
LOG2E = 1.4426950408889634


def _params(sem, vmem=VMEM_LIMIT):
    return pltpu.CompilerParams(dimension_semantics=sem, vmem_limit_bytes=vmem)


def _silu(x):
    return x / (1.0 + jnp.exp(-x))


def _matvec_kernel(c_ref, w_ref, b_ref, o_ref):
    c = c_ref[...]
    ca = jnp.broadcast_to(_silu(c), (8, c.shape[1])).astype(BF16)
    acc = jnp.dot(ca, w_ref[...].astype(BF16), preferred_element_type=F32)
    o_ref[...] = acc[0:1, :] + b_ref[...]


def _ada_matvec(c, w, b, tn=1024):
    L, D, N = w.shape
    return pl.pallas_call(
        _matvec_kernel,
        out_shape=jax.ShapeDtypeStruct((L, 1, N), F32),
        grid=(L, N // tn),
        in_specs=[
            pl.BlockSpec((1, D), lambda l, j: (0, 0)),
            pl.BlockSpec((None, D, tn), lambda l, j: (l, 0, j)),
            pl.BlockSpec((None, 1, tn), lambda l, j: (l, 0, j)),
        ],
        out_specs=pl.BlockSpec((None, 1, tn), lambda l, j: (l, 0, j)),
        compiler_params=_params(("parallel", "parallel")),
        name="ada_matvec",
    )(c, w, b)


def _rope_table_kernel(pos_ref, fr_ref, fa_ref, cr_ref, sr_ref, ca_ref, sa_ref):
    pos = pos_ref[...]
    ang_r = pos * fr_ref[...]
    cr_ref[...] = jnp.cos(ang_r)
    sr_ref[...] = jnp.sin(ang_r)
    ang_a = pos * fa_ref[...]
    lane = lax.broadcasted_iota(jnp.int32, ang_a.shape, 1)
    ca_ref[...] = jnp.cos(ang_a)
    sa_ref[...] = jnp.where(lane < ROT_PARTNER, -jnp.sin(ang_a), jnp.sin(ang_a))


def _rope_tables(pos_col, inv_freq_ret, inv_freq_attn, tb=1024):
    S = pos_col.shape[0]
    tab = jax.ShapeDtypeStruct((S, LANES), F32)
    row = pl.BlockSpec((tb, LANES), lambda i: (i, 0))
    vec = pl.BlockSpec((1, LANES), lambda i: (0, 0))
    return pl.pallas_call(
        _rope_table_kernel,
        out_shape=(tab,) * 4,
        grid=(S // tb,),
        in_specs=[pl.BlockSpec((tb, 1), lambda i: (i, 0)), vec, vec],
        out_specs=(row,) * 4,
        compiler_params=_params(("parallel",)),
        name="rope_tables",
    )(pos_col, inv_freq_ret, inv_freq_attn)


PAIR_TILE = KV_HEADS * HEAD_DIM


def _cast_block(w_ref, o_ref, pair_period=0):
    if pair_period == 0:
        o_ref[...] = w_ref[...].astype(o_ref.dtype)
        return
    half = ROT_DIM // 2
    shift = ROT_PARTNER - half
    for ct in range(w_ref.shape[1] // PAIR_TILE):
        if ct % pair_period != 0:
            cols = slice(ct * PAIR_TILE, (ct + 1) * PAIR_TILE)
            o_ref[:, cols] = w_ref[:, cols].astype(o_ref.dtype)
            continue
        for hh in range(PAIR_TILE // HEAD_DIM):
            cols = slice(ct * PAIR_TILE + hh * HEAD_DIM, ct * PAIR_TILE + (hh + 1) * HEAD_DIM)
            x = w_ref[:, cols]
            lane = lax.broadcasted_iota(jnp.int32, x.shape, 1)
            from_hi = (lane >= half) & (lane < ROT_DIM)
            from_lo = (lane >= ROT_PARTNER) & (lane < ROT_PARTNER + half)
            y = jnp.where(from_hi, pltpu.roll(x, HEAD_DIM - shift, 1), jnp.where(from_lo, pltpu.roll(x, shift, 1), x))
            o_ref[:, cols] = y.astype(o_ref.dtype)


def _prep_weight_kernel(w_ref, o_ref):
    _cast_block(w_ref, o_ref)


def _prep_weight(w, layer=0, *, bw=512, src_block=None):
    _, K, N = w.shape
    src = src_block if src_block is not None else (lambda j: j)
    return pl.pallas_call(
        _prep_weight_kernel,
        out_shape=jax.ShapeDtypeStruct((K, N), BF16),
        grid=(N // bw,),
        in_specs=[pl.BlockSpec((None, K, bw), lambda j: (layer, 0, src(j)))],
        out_specs=pl.BlockSpec((K, bw), lambda j: (0, j)),
        compiler_params=_params(("parallel",)),
        name="prep_weight",
    )(w)


class _SideCast:
    def __init__(self, w, layer=0, pair_period=0):
        self.w, self.layer, self.pair_period = w, layer, pair_period

    def plan(self, grid):
        _, K, N = self.w.shape
        n_steps, nj, layer = grid[0] * grid[1], grid[1], self.layer
        bf16_rows = 16
        n_chunks = max(n for n in range(1, n_steps + 1) if K % (n * bf16_rows) == 0)
        rows = K // n_chunks
        chunk = lambda i, j: jnp.minimum(i * nj + j, n_chunks - 1)
        return (pl.BlockSpec((None, rows, N), lambda i, j: (layer, chunk(i, j), 0)),
                pl.BlockSpec((rows, N), lambda i, j: (chunk(i, j), 0)),
                jax.ShapeDtypeStruct((K, N), BF16))


NORM_ROWS = 128


def _norm_modulate_into(x_ref, g_ref, sh_ref, sc_ref, h_ref, row0=0):
    a = g_ref[...] * (1.0 + sc_ref[...])
    b = sh_ref[...]

    def body(c, carry):
        off = pl.multiple_of(c * NORM_ROWS, NORM_ROWS)
        xc = x_ref[pl.ds(off, NORM_ROWS), :]
        ms = jnp.mean(xc * xc, axis=-1, keepdims=True)
        h_ref[pl.ds(pl.multiple_of(row0 + off, NORM_ROWS), NORM_ROWS), :] = (
            xc * lax.rsqrt(ms + EPS) * a + b).astype(BF16)
        return carry

    lax.fori_loop(0, x_ref.shape[0] // NORM_ROWS, body, 0)


RET_HEAD_COLS = 2 * RET_DK + 2 * RET_DV


def _ret_head_src_block(ob):
    per_head = RET_HEAD_COLS // RET_DK
    vblk = RET_DV // RET_DK
    h, p = ob // per_head, ob % per_head
    q_src = h
    k_src = RET_HEADS + h
    v_src = 2 * RET_HEADS + vblk * h + (p - 2)
    g_src = (2 + vblk) * RET_HEADS + vblk * h + (p - 2 - vblk)
    return jnp.where(p == 0, q_src, jnp.where(p == 1, k_src, jnp.where(p < 2 + vblk, v_src, g_src)))


def _ret_in_kernel(x_ref, g_ref, sh_ref, sc_ref, w_ref, cos_ref, sin_ref, sw_ref, o_ref, so_ref, h_ref):
    _cast_block(sw_ref, so_ref)

    @pl.when(pl.program_id(1) == 0)
    def _():
        _norm_modulate_into(x_ref, g_ref, sh_ref, sc_ref, h_ref)

    h = h_ref[...]
    half = RET_DK // 2
    v0, g0 = 2 * RET_DK, 2 * RET_DK + RET_DV

    qk = jnp.dot(h, w_ref[:, :v0], preferred_element_type=F32)
    for lo, scale in ((0, 1.0), (RET_DK, RET_DK ** -0.5)):
        cos = cos_ref[...] * scale
        sin = sin_ref[...] * scale
        a1 = qk[:, lo:lo + half]
        a2 = qk[:, lo + half:lo + RET_DK]
        o_ref[:, lo:lo + half] = (a1 * cos - a2 * sin).astype(o_ref.dtype)
        o_ref[:, lo + half:lo + RET_DK] = (a2 * cos + a1 * sin).astype(o_ref.dtype)

    gate = jnp.dot(h, w_ref[:, g0:], preferred_element_type=F32)
    o_ref[:, g0:] = _silu(gate).astype(o_ref.dtype)
    v = jnp.dot(h, w_ref[:, v0:g0], preferred_element_type=F32)
    o_ref[:, v0:g0] = v.astype(o_ref.dtype)


def _ret_in_proj(x, g, sh, sc, w, cos, sin, side, tm=1024):
    S, D = x.shape
    N = w.shape[1]
    tn = RET_HEAD_COLS
    grid = (S // tm, N // tn)
    vec = pl.BlockSpec((1, D), lambda i, j: (0, 0))
    tab = pl.BlockSpec((tm, LANES), lambda i, j: (i, 0))
    side_in, side_out, side_shape = side.plan(grid)
    return pl.pallas_call(
        _ret_in_kernel,
        out_shape=(jax.ShapeDtypeStruct((S, N), BF16), side_shape),
        grid=grid,
        in_specs=[pl.BlockSpec((tm, D), lambda i, j: (i, 0)), vec, vec, vec,
                  pl.BlockSpec((D, tn), lambda i, j: (0, j)), tab, tab, side_in],
        out_specs=(pl.BlockSpec((tm, tn), lambda i, j: (i, j)), side_out),
        scratch_shapes=[pltpu.VMEM((tm, D), BF16)],
        compiler_params=_params(("arbitrary", "arbitrary")),
        name="ret_in_proj",
    )(x, g, sh, sc, w, cos, sin, side.w)


SPAN_NORM_STEPS = 4
SPAN_SUB = 256
SPAN_TN = 1024
SPAN_ACC_BUFS = 2


def _span_chunks(d):
    B = ATTN_BLOCK
    out = []
    for s in range(SPAN // (B * d)):
        for r in range(d):
            u = s * d + r
            out.append((pl.ds(s * B * d + r, B, stride=d) if d > 1 else pl.ds(u * B, B), pl.ds(u * B, B)))
    return out


def _span_proj_kernel(x_ref, g_ref, sh_ref, sc_ref, w_ref, c_ref, s_ref, sw_ref, o_ref, so_ref,
                      h_ref, tab_ref, *acc_refs, out_scale, steps_per_group, rope_cols):
    j = pl.program_id(1)
    jm = j - SPAN_NORM_STEPS
    _cast_block(sw_ref, so_ref)

    @pl.when(jm < 0)
    def _():
        _norm_modulate_into(x_ref, g_ref, sh_ref, sc_ref, h_ref, row0=j * (SPAN // SPAN_NORM_STEPS))

    def run(d):
        chunks = _span_chunks(d)

        @pl.when(jm % steps_per_group == 0)
        def _():
            for src, dst in chunks:
                tab_ref[0, dst, :] = c_ref[src, :] * out_scale
                tab_ref[1, dst, :] = s_ref[src, :] * out_scale

        n_slab = SPAN_SUB // LANES
        for t in range(w_ref.shape[1] // SPAN_SUB):
            col0 = t * SPAN_SUB
            acc = jnp.dot(h_ref[...], w_ref[:, col0:col0 + SPAN_SUB], preferred_element_type=F32)
            slabs = acc_refs[(t % SPAN_ACC_BUFS) * n_slab:(t % SPAN_ACC_BUFS + 1) * n_slab]
            for c in range(n_slab):
                slabs[c][...] = acc[:, c * LANES:(c + 1) * LANES]
            rope = col0 < rope_cols
            for src, dst in chunks:
                if rope:
                    ct = tab_ref[0, dst, :]
                    st = tab_ref[1, dst, :]
                for c in range(n_slab):
                    xr = slabs[c][src, :]
                    y = xr * ct + pltpu.roll(xr, ROT_PARTNER, 1) * st if rope else xr
                    o_ref[dst, col0 + c * LANES:col0 + (c + 1) * LANES] = y.astype(o_ref.dtype)

    for gi, (_, d) in enumerate(DIL_GROUPS):
        pl.when((jm >= 0) & (jm // steps_per_group == gi))(functools.partial(run, d))


def _span_proj(x, g, sh, sc, w, c, s, side, *, out_scale, rope_cols):
    S, D = x.shape
    N = w.shape[1]
    tm, tn, ns = SPAN, SPAN_TN, SPAN_NORM_STEPS
    n_mm = N // tn
    grid = (S // tm, ns + n_mm)
    vec = pl.BlockSpec((1, D), lambda i, j: (0, 0))
    tab = pl.BlockSpec((tm, LANES), lambda i, j: (i, 0))
    mm_step = lambda j: jnp.maximum(j - ns, 0)
    side_in, side_out, side_shape = side.plan(grid)
    return pl.pallas_call(
        functools.partial(_span_proj_kernel, out_scale=out_scale,
                          steps_per_group=n_mm // len(DIL_GROUPS), rope_cols=rope_cols),
        out_shape=(jax.ShapeDtypeStruct((S, N), BF16), side_shape),
        grid=grid,
        in_specs=[pl.BlockSpec((tm // ns, D), lambda i, j: (i * ns + jnp.minimum(j, ns - 1), 0)), vec, vec, vec,
                  pl.BlockSpec((D, tn), lambda i, j: (0, mm_step(j))), tab, tab, side_in],
        out_specs=(pl.BlockSpec((tm, tn), lambda i, j: (i, mm_step(j))), side_out),
        scratch_shapes=([pltpu.VMEM((tm, D), BF16), pltpu.VMEM((2, tm, LANES), F32)]
                        + [pltpu.VMEM((tm, LANES), F32)] * (SPAN_ACC_BUFS * SPAN_SUB // LANES)),
        compiler_params=_params(("arbitrary", "arbitrary")),
        name="span_proj",
    )(x, g, sh, sc, w, c, s, side.w)


def _with_side(side, grid, in_specs, out_spec, out_shape, operands):
    if side is None:
        return in_specs, out_spec, out_shape, operands
    side_in, side_out, side_shape = side.plan(grid)
    return in_specs + [side_in], (out_spec, side_out), (out_shape, side_shape), operands + (side.w,)


def _ffn_in_kernel(x_ref, g_ref, sh_ref, sc_ref, wg_ref, wu_ref, *rest, side_pair):
    if side_pair is None:
        o_ref, h_ref = rest
    else:
        sw_ref, o_ref, so_ref, h_ref = rest
        _cast_block(sw_ref, so_ref, side_pair)

    j = pl.program_id(1)

    @pl.when(j < FFN_NORM_STEPS)
    def _():
        _norm_modulate_into(x_ref, g_ref, sh_ref, sc_ref, h_ref, row0=j * x_ref.shape[0])

    @pl.when(j >= FFN_NORM_STEPS)
    def _():
        h = h_ref[...]
        gate = jnp.dot(h, wg_ref[...], preferred_element_type=F32)
        up = jnp.dot(h, wu_ref[...], preferred_element_type=F32)
        o_ref[...] = (_silu(gate) * up).astype(o_ref.dtype)


FFN_NORM_STEPS = 4


def _ffn_in_proj(x, g, sh, sc, w, side=None, tm=2048, tn=512):
    S, D = x.shape
    F = w.shape[1] // 2
    nj = F // tn
    ns = FFN_NORM_STEPS
    grid = (S // tm, ns + nj)
    vec = pl.BlockSpec((1, D), lambda i, j: (0, 0))
    mm_step = lambda j: jnp.maximum(j - ns, 0)
    in_specs, out_specs, out_shape, operands = _with_side(
        side, grid,
        [pl.BlockSpec((tm // ns, D), lambda i, j: (i * ns + jnp.minimum(j, ns - 1), 0)), vec, vec, vec,
         pl.BlockSpec((D, tn), lambda i, j: (0, mm_step(j))),
         pl.BlockSpec((D, tn), lambda i, j: (0, mm_step(j) + nj))],
        pl.BlockSpec((tm, tn), lambda i, j: (i, mm_step(j))),
        jax.ShapeDtypeStruct((S, F), BF16),
        (x, g, sh, sc, w, w))
    return pl.pallas_call(
        functools.partial(_ffn_in_kernel, side_pair=None if side is None else side.pair_period),
        out_shape=out_shape,
        grid=grid,
        in_specs=in_specs,
        out_specs=out_specs,
        scratch_shapes=[pltpu.VMEM((tm, D), BF16)],
        compiler_params=_params(("arbitrary", "arbitrary")),
        name="ffn_in_proj",
    )(*operands)


def _out_proj_kernel(a_ref, w_ref, r_ref, gt_ref, *rest, side_pair):
    if side_pair is None:
        (o_ref,) = rest
    else:
        sw_ref, o_ref, so_ref = rest
        _cast_block(sw_ref, so_ref, side_pair)
    acc = jnp.dot(a_ref[...], w_ref[...], preferred_element_type=F32)
    o_ref[...] = r_ref[...] + gt_ref[...] * acc


def _out_proj(a, w, resid, gate, side=None, tm=1024, tn=512):
    S, K = a.shape
    N = w.shape[1]
    grid = (S // tm, N // tn)
    in_specs, out_specs, out_shape, operands = _with_side(
        side, grid,
        [pl.BlockSpec((tm, K), lambda i, j: (i, 0)),
         pl.BlockSpec((K, tn), lambda i, j: (0, j)),
         pl.BlockSpec((tm, tn), lambda i, j: (i, j)),
         pl.BlockSpec((1, tn), lambda i, j: (0, j))],
        pl.BlockSpec((tm, tn), lambda i, j: (i, j)),
        jax.ShapeDtypeStruct((S, N), F32),
        (a, w, resid, gate))
    return pl.pallas_call(
        functools.partial(_out_proj_kernel, side_pair=None if side is None else side.pair_period),
        out_shape=out_shape,
        grid=grid,
        in_specs=in_specs,
        out_specs=out_specs,
        compiler_params=_params(("arbitrary", "arbitrary")),
        name="out_proj",
    )(*operands)


def _retention_kernel(lg_ref, x_ref, sw_ref, o_ref, so_ref, r_ref):
    _cast_block(sw_ref, so_ref)

    @pl.when(pl.program_id(1) == 0)
    def _():
        r_ref[...] = jnp.zeros_like(r_ref)

    C = RET_KERNEL_CHUNK
    k0, v0, g0 = RET_DK, 2 * RET_DK, 2 * RET_DK + RET_DV
    ri = lax.broadcasted_iota(jnp.int32, (C, C), 0)
    ci = lax.broadcasted_iota(jnp.int32, (C, C), 1)
    diff = (ri - ci).astype(F32)
    idx = lax.broadcasted_iota(jnp.int32, (C, 1), 0).astype(F32)
    decays = []
    for hh in range(RET_HEADS_PER_STEP):
        lg = lg_ref[pl.program_id(0) * RET_HEADS_PER_STEP + hh]
        decays.append((jnp.where(diff >= 0, jnp.exp(lg * jnp.maximum(diff, 0.0)), 0.0),
                       jnp.exp(lg * (idx + 1.0)),
                       jnp.exp(lg * (C - 1.0 - idx)),
                       jnp.exp(jnp.full((1, 1), lg * C, F32))))

    def body(c, carry):
        rows = pl.ds(pl.multiple_of(c * C, C), C)
        for hh, (dmask, q_decay, k_decay, chunk_decay) in enumerate(decays):
            x0 = hh * RET_HEAD_COLS
            qc = x_ref[rows, x0:x0 + k0]
            kc = x_ref[rows, x0 + k0:x0 + v0]
            vc = x_ref[rows, x0 + v0:x0 + g0]
            state = r_ref[hh]
            scores = lax.dot_general(qc, kc, (((1,), (1,)), ((), ())), preferred_element_type=F32) * dmask
            o = (jnp.dot(scores.astype(BF16), vc, preferred_element_type=F32)
                 + jnp.dot(qc, state.astype(BF16), preferred_element_type=F32) * q_decay)
            kd = (kc.astype(F32) * k_decay).astype(BF16)
            r_ref[hh] = state * chunk_decay + lax.dot_general(
                kd, vc, (((0,), (0,)), ((), ())), preferred_element_type=F32)
            y = o * lax.rsqrt(jnp.mean(o * o, axis=-1, keepdims=True) + EPS)
            gate = x_ref[rows, x0 + g0:x0 + RET_HEAD_COLS].astype(F32)
            o_ref[rows, hh * RET_DV:(hh + 1) * RET_DV] = (gate * y).astype(o_ref.dtype)
        return carry

    lax.fori_loop(0, x_ref.shape[0] // C, body, 0, unroll=2)


def _retention(qkvg, log_g, side, tb=1024):
    S = qkvg.shape[0]
    hps = RET_HEADS_PER_STEP
    grid = (RET_HEADS // hps, S // tb)
    side_in, side_out, side_shape = side.plan(grid)
    return pl.pallas_call(
        _retention_kernel,
        out_shape=(jax.ShapeDtypeStruct((S, RET_HEADS * RET_DV), BF16), side_shape),
        grid=grid,
        in_specs=[pl.BlockSpec(memory_space=pltpu.SMEM),
                  pl.BlockSpec((tb, hps * RET_HEAD_COLS), lambda h, t: (t, h)), side_in],
        out_specs=(pl.BlockSpec((tb, hps * RET_DV), lambda h, t: (t, h)), side_out),
        scratch_shapes=[pltpu.VMEM((hps, RET_DK, RET_DV), F32)],
        compiler_params=_params(("arbitrary", "arbitrary")),
        name="retention",
    )(log_g, qkvg, side.w)


UNIT_UNROLL = 8


def _attn_unit(q_ref, row, k_own, v_own, k_prev, v_prev, bias):
    rep = Q_HEADS // KV_HEADS
    B = ATTN_BLOCK
    q4 = jnp.concatenate([q_ref[pl.ds(row, B), r * HEAD_DIM:(r + 1) * HEAD_DIM] for r in range(rep)], axis=0)
    k_cat = jnp.concatenate([k_prev, k_own], axis=0)
    v_cat = jnp.concatenate([v_prev, v_own], axis=0)
    v_ext = jnp.concatenate([v_cat, jnp.ones_like(v_cat)], axis=1)
    s = lax.dot_general(q4, k_cat, (((1,), (1,)), ((), ())), preferred_element_type=F32) + bias
    m = jnp.max(s, axis=-1, keepdims=True)
    p = jnp.exp2(s - m).astype(BF16)
    ov = jnp.dot(p, v_ext, preferred_element_type=F32)
    l = ov[:, HEAD_DIM:]
    return ov[:, :HEAD_DIM] / l, m + jnp.log2(l)


def _dilated_attn_kernel(q0_ref, q1_ref, q2_ref, k0_ref, v0_ref, k1_ref, v1_ref, k2_ref, v2_ref,
                         hk0_ref, hv0_ref, hk1_ref, hv1_ref, hk2_ref, hv2_ref, sw_ref,
                         o_ref, so_ref, onat_ref, lnat_ref, bias_ref):
    _cast_block(sw_ref, so_ref)
    i = pl.program_id(0)
    B = ATTN_BLOCK
    rep = Q_HEADS // KV_HEADS
    n_units = SPAN // B
    d1, d2 = DIL_GROUPS[1][1], DIL_GROUPS[2][1]

    @pl.when((i == 0) & (pl.program_id(1) == 0))
    def _():
        ri = lax.broadcasted_iota(jnp.int32, (rep * B, 2 * B), 0) % B
        ci = lax.broadcasted_iota(jnp.int32, (rep * B, 2 * B), 1)
        own = (ci >= B) & (ci - B <= ri)
        prev = (ci < B) & (ci >= ri)
        bias_ref[0] = jnp.where(own | prev, 0.0, MASKED)
        bias_ref[1] = jnp.where(own, 0.0, MASKED)

    def scatter(slot, dst, o, lse):
        for hq in range(rep):
            onat_ref[slot, hq, dst, :] = o[hq * B:(hq + 1) * B, :]
            lnat_ref[slot, hq, dst, :] = lse[hq * B:(hq + 1) * B, :]

    def widest_body(r, carry):
        row = pl.multiple_of(r * B, B)
        rows = pl.ds(row, B)
        bias = bias_ref[(i == 0).astype(jnp.int32)]
        o, lse = _attn_unit(q2_ref, row, k2_ref[rows, :], v2_ref[rows, :], hk2_ref[rows, :], hv2_ref[rows, :], bias)
        scatter(1, pl.ds(r, B, stride=d2), o, lse)
        return carry

    lax.fori_loop(0, n_units, widest_body, 0, unroll=UNIT_UNROLL)

    def middle_body(u, carry):
        s = u // d1
        r = u % d1
        row = pl.multiple_of(u * B, B)
        rows = pl.ds(row, B)
        prow = pl.ds(pl.multiple_of(jnp.maximum(u - d1, 0) * B, B), B)
        hrow = pl.ds(pl.multiple_of(r * B, B), B)
        first = s == 0
        k_prev = jnp.where(first, hk1_ref[hrow, :], k1_ref[prow, :])
        v_prev = jnp.where(first, hv1_ref[hrow, :], v1_ref[prow, :])
        bias = bias_ref[(first & (i == 0)).astype(jnp.int32)]
        o, lse = _attn_unit(q1_ref, row, k1_ref[rows, :], v1_ref[rows, :], k_prev, v_prev, bias)
        scatter(0, pl.ds(s * (B * d1) + r, B, stride=d1), o, lse)
        return carry

    lax.fori_loop(0, n_units, middle_body, 0, unroll=UNIT_UNROLL)

    def dense_body(b, carry):
        row = pl.multiple_of(b * B, B)
        rows = pl.ds(row, B)
        prow = pl.ds(pl.multiple_of(jnp.maximum(b - 1, 0) * B, B), B)
        first = b == 0
        k_prev = jnp.where(first, hk0_ref[...], k0_ref[prow, :])
        v_prev = jnp.where(first, hv0_ref[...], v0_ref[prow, :])
        bias = bias_ref[(first & (i == 0)).astype(jnp.int32)]
        o0, l0 = _attn_unit(q0_ref, row, k0_ref[rows, :], v0_ref[rows, :], k_prev, v_prev, bias)
        for hq in range(rep):
            a0 = o0[hq * B:(hq + 1) * B, :]
            e0 = l0[hq * B:(hq + 1) * B, :]
            e1 = lnat_ref[0, hq, rows, :]
            e2 = lnat_ref[1, hq, rows, :]
            mx = jnp.maximum(jnp.maximum(e0, e1), e2)
            w0, w1, w2 = jnp.exp2(e0 - mx), jnp.exp2(e1 - mx), jnp.exp2(e2 - mx)
            merged = (w0 * a0 + w1 * onat_ref[0, hq, rows, :] + w2 * onat_ref[1, hq, rows, :]) / (w0 + w1 + w2)
            o_ref[rows, hq * HEAD_DIM:(hq + 1) * HEAD_DIM] = merged.astype(o_ref.dtype)
        return carry

    lax.fori_loop(0, n_units, dense_body, 0, unroll=UNIT_UNROLL)


def _dilated_attention(q_all, kv_all, side):
    S = q_all.shape[0]
    rep = Q_HEADS // KV_HEADS
    qw = rep * HEAD_DIM
    per_group = 2 * KV_HEADS
    B = ATTN_BLOCK

    def q_spec(g):
        return pl.BlockSpec((SPAN, qw), lambda i, hd: (i, g * KV_HEADS + hd))

    def kv_spec(g, is_v):
        return pl.BlockSpec((SPAN, HEAD_DIM), lambda i, hd: (i, g * per_group + is_v * KV_HEADS + hd))

    def halo_spec(g, is_v):
        rows = B * DIL_GROUPS[g][1]
        n = SPAN // rows
        return pl.BlockSpec((rows, HEAD_DIM),
                            lambda i, hd: (jnp.maximum(i * n - 1, 0), g * per_group + is_v * KV_HEADS + hd))

    groups = range(len(DIL_GROUPS))
    in_specs = ([q_spec(g) for g in groups]
                + [kv_spec(g, v) for g in groups for v in (0, 1)]
                + [halo_spec(g, v) for g in groups for v in (0, 1)])
    grid = (S // SPAN, KV_HEADS)
    side_in, side_out, side_shape = side.plan(grid)
    return pl.pallas_call(
        _dilated_attn_kernel,
        out_shape=(jax.ShapeDtypeStruct((S, Q_HEADS * HEAD_DIM), BF16), side_shape),
        grid=grid,
        in_specs=in_specs + [side_in],
        out_specs=(pl.BlockSpec((SPAN, qw), lambda i, hd: (i, hd)), side_out),
        scratch_shapes=[pltpu.VMEM((2, rep, SPAN, LANES), F32),
                        pltpu.VMEM((2, rep, SPAN, LANES), F32),
                        pltpu.VMEM((2, rep * B, 2 * B), F32)],
        compiler_params=_params(("arbitrary", "arbitrary")),
        name="dilated_attn",
    )(*([q_all] * 3 + [kv_all] * 12 + [side.w]))


def _final_norm_kernel(x_ref, g_ref, o_ref):
    x = x_ref[...]
    ms = jnp.mean(x * x, axis=-1, keepdims=True)
    o_ref[...] = x * lax.rsqrt(ms + EPS) * g_ref[...]


def _final_norm(x, g, tm=512):
    S, D = x.shape
    return pl.pallas_call(
        _final_norm_kernel,
        out_shape=jax.ShapeDtypeStruct((S, D), F32),
        grid=(S // tm,),
        in_specs=[pl.BlockSpec((tm, D), lambda i: (i, 0)), pl.BlockSpec((1, D), lambda i: (0, 0))],
        out_specs=pl.BlockSpec((tm, D), lambda i: (i, 0)),
        compiler_params=_params(("parallel",)),
        name="final_norm",
    )(x, g)


def kernel(x, c, positions, ada_w, ada_b, norm_g, ffn_w_in, ffn_w_out, ret_w_in, ret_w_out, kv_norm_g, kv_ada_w, kv_ada_b, kv_w, attn_w_q, attn_w_out, final_norm_g):
    B, S, D = x.shape
    assert B == 1 and S % SPAN == 0
    assert all(w // d == ATTN_BLOCK for w, d in DIL_GROUPS) and DIL_GROUPS[0][1] == 1
    xs = x.reshape(S, D)

    mod = _ada_matvec(c, ada_w, ada_b[:, None, :])
    kv_mod = _ada_matvec(c, kv_ada_w[None], kv_ada_b[None, None, :])[0]

    def mod_vec(layer, idx):
        return mod[layer, :, idx * D:(idx + 1) * D]

    inv_freq_ret = 1.0 / (RET_THETA ** jnp.linspace(0.0, 1.0, RET_DK // 2, dtype=F32))
    inv_freq_attn = ROPE_THETA ** (-jnp.arange(0, ROT_DIM, 2, dtype=F32) / ROT_DIM)
    gap = jnp.zeros((ROT_PARTNER - ROT_DIM // 2,), F32)
    inv_freq_attn = jnp.concatenate([inv_freq_attn, gap, inv_freq_attn, gap])
    pos_col = positions.reshape(S, 1).astype(F32)
    cos_r, sin_r, cos_a, sin_a = _rope_tables(pos_col, inv_freq_ret[None, :], inv_freq_attn[None, :])

    w_ret_in = _prep_weight(ret_w_in, bw=RET_DK, src_block=_ret_head_src_block)
    qkvg, w_ret_out = _ret_in_proj(xs, norm_g[0, 0][None], mod_vec(0, 0), mod_vec(0, 1), w_ret_in, cos_r, sin_r,
                                   _SideCast(ret_w_out))
    log_g = jnp.log1p(-(2.0 ** (-5.0 - jnp.arange(RET_HEADS, dtype=F32))))
    y, w_ffn_in0 = _retention(qkvg, log_g, _SideCast(ffn_w_in, 0))
    xs, w_kv = _out_proj(y, w_ret_out, xs, mod_vec(0, 2), _SideCast(kv_w[None], pair_period=2))
    hid, w_ffn_out0 = _ffn_in_proj(xs, norm_g[0, 1][None], mod_vec(0, 3), mod_vec(0, 4), w_ffn_in0,
                                   _SideCast(ffn_w_out, 0))
    xs, w_q = _out_proj(hid, w_ffn_out0, xs, mod_vec(0, 5), _SideCast(attn_w_q, pair_period=1))

    kv_all, w_attn_out = _span_proj(xs, kv_norm_g[None], kv_mod[:, :D], kv_mod[:, D:], w_kv, cos_a, sin_a,
                                    _SideCast(attn_w_out), out_scale=1.0, rope_cols=PAIR_TILE)

    q_all, w_ffn_in1 = _span_proj(xs, norm_g[1, 0][None], mod_vec(1, 0), mod_vec(1, 1), w_q, cos_a, sin_a,
                                  _SideCast(ffn_w_in, 1), out_scale=HEAD_DIM ** -0.5 * LOG2E, rope_cols=SPAN_TN)
    attn, w_ffn_out1 = _dilated_attention(q_all, kv_all, _SideCast(ffn_w_out, 1))
    xs = _out_proj(attn, w_attn_out, xs, mod_vec(1, 2))
    hid = _ffn_in_proj(xs, norm_g[1, 1][None], mod_vec(1, 3), mod_vec(1, 4), w_ffn_in1)
    xs = _out_proj(hid, w_ffn_out1, xs, mod_vec(1, 5))

    return _final_norm(xs, final_norm_g[None]).reshape(B, S, D)
```

```python
import functools

import jax
import jax.numpy as jnp
from jax import lax
from jax.experimental import pallas as pl
from jax.experimental.pallas import tpu as pltpu

F32 = jnp.float32
BF16 = jnp.bfloat16

EPS = 1e-6
RET_HEADS = 8
RET_DK = 256
RET_DV = 512
RET_CHUNK = 128
RET_KERNEL_CHUNK = 256
RET_HEADS_PER_STEP = 2
RET_THETA = 10000.0
DIL_GROUPS = ((128, 1), (512, 4), (2048, 16))
HEAD_DIM = 128
Q_HEADS = 16
KV_HEADS = 4
ROT_DIM = 32
ROPE_THETA = 500000.0
ATTN_BLOCK = 128
N_MOD = 6

LANES = 128
VMEM_LIMIT = 56 * 1024 * 1024
SPAN = ATTN_BLOCK * max(d for _, d in DIL_GROUPS)
ROT_PARTNER = LANES // 2
MASKED = -1e30
LOG2E = 1.4426950408889634


def _params(sem, vmem=VMEM_LIMIT):
    return pltpu.CompilerParams(dimension_semantics=sem, vmem_limit_bytes=vmem)


def _silu(x):
    return x / (1.0 + jnp.exp(-x))


def _matvec_kernel(c_ref, w_ref, b_ref, o_ref):
    c = c_ref[...]
    ca = jnp.broadcast_to(_silu(c), (8, c.shape[1])).astype(BF16)
    acc = jnp.dot(ca, w_ref[...].astype(BF16), preferred_element_type=F32)
    o_ref[...] = acc[0:1, :] + b_ref[...]


def _ada_matvec(c, w, b, tn=1024):
    L, D, N = w.shape
    return pl.pallas_call(
        _matvec_kernel,
        out_shape=jax.ShapeDtypeStruct((L, 1, N), F32),
        grid=(L, N // tn),
        in_specs=[
            pl.BlockSpec((1, D), lambda l, j: (0, 0)),
            pl.BlockSpec((None, D, tn), lambda l, j: (l, 0, j)),
            pl.BlockSpec((None, 1, tn), lambda l, j: (l, 0, j)),
        ],
        out_specs=pl.BlockSpec((None, 1, tn), lambda l, j: (l, 0, j)),
        compiler_params=_params(("parallel", "parallel")),
        name="ada_matvec",
    )(c, w, b)


def _rope_block(pos_ref, f_ref, cos_ref, sin_ref, paired):
    ang = pos_ref[...] * f_ref[...]
    cos_ref[...] = jnp.cos(ang)
    sin = jnp.sin(ang)
    if paired:
        lane = lax.broadcasted_iota(jnp.int32, ang.shape, 1)
        sin = jnp.where(lane < ROT_PARTNER, -sin, sin)
    sin_ref[...] = sin


def _rope_table_kernel(pos_ref, fr_ref, fa_ref, cr_ref, sr_ref, ca_ref, sa_ref):
    _rope_block(pos_ref, fr_ref, cr_ref, sr_ref, paired=False)
    _rope_block(pos_ref, fa_ref, ca_ref, sa_ref, paired=True)


def _rope_tables(pos_col, inv_freq_ret, inv_freq_attn, tb=1024):
    S = pos_col.shape[0]
    tab = jax.ShapeDtypeStruct((S, LANES), F32)
    row = pl.BlockSpec((tb, LANES), lambda i: (i, 0))
    vec = pl.BlockSpec((1, LANES), lambda i: (0, 0))
    return pl.pallas_call(
        _rope_table_kernel,
        out_shape=(tab,) * 4,
        grid=(S // tb,),
        in_specs=[pl.BlockSpec((tb, 1), lambda i: (i, 0)), vec, vec],
        out_specs=(row,) * 4,
        compiler_params=_params(("parallel",)),
        name="rope_tables",
    )(pos_col, inv_freq_ret, inv_freq_attn)


PAIR_TILE = KV_HEADS * HEAD_DIM


def _cast_block(w_ref, o_ref, pair_period=0):
    if pair_period == 0:
        o_ref[...] = w_ref[...].astype(o_ref.dtype)
        return
    half = ROT_DIM // 2
    shift = ROT_PARTNER - half
    for ct in range(w_ref.shape[1] // PAIR_TILE):
        if ct % pair_period != 0:
            cols = slice(ct * PAIR_TILE, (ct + 1) * PAIR_TILE)
            o_ref[:, cols] = w_ref[:, cols].astype(o_ref.dtype)
            continue
        for hh in range(PAIR_TILE // HEAD_DIM):
            cols = slice(ct * PAIR_TILE + hh * HEAD_DIM, ct * PAIR_TILE + (hh + 1) * HEAD_DIM)
            x = w_ref[:, cols]
            lane = lax.broadcasted_iota(jnp.int32, x.shape, 1)
            from_hi = (lane >= half) & (lane < ROT_DIM)
            from_lo = (lane >= ROT_PARTNER) & (lane < ROT_PARTNER + half)
            y = jnp.where(from_hi, pltpu.roll(x, HEAD_DIM - shift, 1), jnp.where(from_lo, pltpu.roll(x, shift, 1), x))
            o_ref[:, cols] = y.astype(o_ref.dtype)


def _prep_weight_kernel(w_ref, o_ref):
    _cast_block(w_ref, o_ref)


def _prep_weight(w, layer=0, *, bw=512, src_block=None):
    _, K, N = w.shape
    src = src_block if src_block is not None else (lambda j: j)
    return pl.pallas_call(
        _prep_weight_kernel,
        out_shape=jax.ShapeDtypeStruct((K, N), BF16),
        grid=(N // bw,),
        in_specs=[pl.BlockSpec((None, K, bw), lambda j: (layer, 0, src(j)))],
        out_specs=pl.BlockSpec((K, bw), lambda j: (0, j)),
        compiler_params=_params(("parallel",)),
        name="prep_weight",
    )(w)


class _SideCast:
    def __init__(self, w, layer=0, pair_period=0):
        self.w, self.layer, self.pair_period = w, layer, pair_period

    def plan(self, grid):
        _, K, N = self.w.shape
        n_steps, nj, layer = grid[0] * grid[1], grid[1], self.layer
        bf16_rows = 16
        n_chunks = max(n for n in range(1, n_steps + 1) if K % (n * bf16_rows) == 0)
        rows = K // n_chunks
        chunk = lambda i, j: jnp.minimum(i * nj + j, n_chunks - 1)
        return (pl.BlockSpec((None, rows, N), lambda i, j: (layer, chunk(i, j), 0)),
                pl.BlockSpec((rows, N), lambda i, j: (chunk(i, j), 0)),
                jax.ShapeDtypeStruct((K, N), BF16))


NORM_ROWS = 128


def _norm_modulate_into(x_ref, g_ref, sh_ref, sc_ref, h_ref, row0=0):
    a = g_ref[...] * (1.0 + sc_ref[...])
    b = sh_ref[...]

    def body(c, carry):
        off = pl.multiple_of(c * NORM_ROWS, NORM_ROWS)
        xc = x_ref[pl.ds(off, NORM_ROWS), :]
        ms = jnp.mean(xc * xc, axis=-1, keepdims=True)
        h_ref[pl.ds(pl.multiple_of(row0 + off, NORM_ROWS), NORM_ROWS), :] = (
            xc * lax.rsqrt(ms + EPS) * a + b).astype(BF16)
        return carry

    lax.fori_loop(0, x_ref.shape[0] // NORM_ROWS, body, 0)


RET_HEAD_COLS = 2 * RET_DK + 2 * RET_DV


def _ret_head_src_block(ob):
    per_head = RET_HEAD_COLS // RET_DK
    vblk = RET_DV // RET_DK
    h, p = ob // per_head, ob % per_head
    q_src = h
    k_src = RET_HEADS + h
    v_src = 2 * RET_HEADS + vblk * h + (p - 2)
    g_src = (2 + vblk) * RET_HEADS + vblk * h + (p - 2 - vblk)
    return jnp.where(p == 0, q_src, jnp.where(p == 1, k_src, jnp.where(p < 2 + vblk, v_src, g_src)))


def _ret_in_kernel(x_ref, g_ref, sh_ref, sc_ref, w_ref, cos_ref, sin_ref, sw_ref, o_ref, so_ref, h_ref):
    @pl.when(pl.program_id(1) == 0)
    def _():
        _norm_modulate_into(x_ref, g_ref, sh_ref, sc_ref, h_ref)

    _cast_block(sw_ref, so_ref)
    h = h_ref[...]
    half = RET_DK // 2
    v0, g0 = 2 * RET_DK, 2 * RET_DK + RET_DV

    qk = jnp.dot(h, w_ref[:, :v0], preferred_element_type=F32)
    for lo, scale in ((0, 1.0), (RET_DK, RET_DK ** -0.5)):
        cos = cos_ref[...] * scale
        sin = sin_ref[...] * scale
        a1 = qk[:, lo:lo + half]
        a2 = qk[:, lo + half:lo + RET_DK]
        o_ref[:, lo:lo + half] = (a1 * cos - a2 * sin).astype(o_ref.dtype)
        o_ref[:, lo + half:lo + RET_DK] = (a2 * cos + a1 * sin).astype(o_ref.dtype)

    gate = jnp.dot(h, w_ref[:, g0:], preferred_element_type=F32)
    o_ref[:, g0:] = _silu(gate).astype(o_ref.dtype)
    v = jnp.dot(h, w_ref[:, v0:g0], preferred_element_type=F32)
    o_ref[:, v0:g0] = v.astype(o_ref.dtype)


def _ret_in_proj(x, g, sh, sc, w, cos, sin, side, tm=1024):
    S, D = x.shape
    N = w.shape[1]
    tn = RET_HEAD_COLS
    grid = (S // tm, N // tn)
    vec = pl.BlockSpec((1, D), lambda i, j: (0, 0))
    tab = pl.BlockSpec((tm, LANES), lambda i, j: (i, 0))
    side_in, side_out, side_shape = side.plan(grid)
    return pl.pallas_call(
        _ret_in_kernel,
        out_shape=(jax.ShapeDtypeStruct((S, N), BF16), side_shape),
        grid=grid,
        in_specs=[pl.BlockSpec((tm, D), lambda i, j: (i, 0)), vec, vec, vec,
                  pl.BlockSpec((D, tn), lambda i, j: (0, j)), tab, tab, side_in],
        out_specs=(pl.BlockSpec((tm, tn), lambda i, j: (i, j)), side_out),
        scratch_shapes=[pltpu.VMEM((tm, D), BF16)],
        compiler_params=_params(("arbitrary", "arbitrary")),
        name="ret_in_proj",
    )(x, g, sh, sc, w, cos, sin, side.w)


SPAN_NORM_STEPS = 4
SPAN_SUB = 256
SPAN_TN = 1024
SPAN_ACC_BUFS = 2


def _span_chunks(d):
    B = ATTN_BLOCK
    out = []
    for s in range(SPAN // (B * d)):
        for r in range(d):
            u = s * d + r
            out.append((pl.ds(s * B * d + r, B, stride=d) if d > 1 else pl.ds(u * B, B), pl.ds(u * B, B)))
    return out


def _span_proj_kernel(x_ref, g_ref, sh_ref, sc_ref, w_ref, c_ref, s_ref, sw_ref, o_ref, so_ref,
                      h_ref, tab_ref, *acc_refs, out_scale, steps_per_group, rope_cols):
    j = pl.program_id(1)
    jm = j - SPAN_NORM_STEPS

    @pl.when(jm < 0)
    def _():
        _cast_block(sw_ref, so_ref)
        _norm_modulate_into(x_ref, g_ref, sh_ref, sc_ref, h_ref, row0=j * (SPAN // SPAN_NORM_STEPS))

    def run(d):
        chunks = _span_chunks(d)

        @pl.when(jm % steps_per_group == 0)
        def _():
            for src, dst in chunks:
                tab_ref[0, dst, :] = c_ref[src, :] * out_scale
                tab_ref[1, dst, :] = s_ref[src, :] * out_scale

        _cast_block(sw_ref, so_ref)
        n_slab = SPAN_SUB // LANES
        for t in range(w_ref.shape[1] // SPAN_SUB):
            col0 = t * SPAN_SUB
            acc = jnp.dot(h_ref[...], w_ref[:, col0:col0 + SPAN_SUB], preferred_element_type=F32)
            slabs = acc_refs[(t % SPAN_ACC_BUFS) * n_slab:(t % SPAN_ACC_BUFS + 1) * n_slab]
            for c in range(n_slab):
                slabs[c][...] = acc[:, c * LANES:(c + 1) * LANES]
            rope = col0 < rope_cols
            for src, dst in chunks:
                if rope:
                    ct = tab_ref[0, dst, :]
                    st = tab_ref[1, dst, :]
                for c in range(n_slab):
                    xr = slabs[c][src, :]
                    y = xr * ct + pltpu.roll(xr, ROT_PARTNER, 1) * st if rope else xr
                    o_ref[dst, col0 + c * LANES:col0 + (c + 1) * LANES] = y.astype(o_ref.dtype)

    for gi, (_, d) in enumerate(DIL_GROUPS):
        pl.when((jm >= 0) & (jm // steps_per_group == gi))(functools.partial(run, d))


def _span_proj(x, g, sh, sc, w, c, s, side, *, out_scale, rope_cols):
    S, D = x.shape
    N = w.shape[1]
    tm, tn, ns = SPAN, SPAN_TN, SPAN_NORM_STEPS
    n_mm = N // tn
    grid = (S // tm, ns + n_mm)
    vec = pl.BlockSpec((1, D), lambda i, j: (0, 0))
    tab = pl.BlockSpec((tm, LANES), lambda i, j: (i, 0))
    mm_step = lambda j: jnp.maximum(j - ns, 0)
    side_in, side_out, side_shape = side.plan(grid)
    return pl.pallas_call(
        functools.partial(_span_proj_kernel, out_scale=out_scale,
                          steps_per_group=n_mm // len(DIL_GROUPS), rope_cols=rope_cols),
        out_shape=(jax.ShapeDtypeStruct((S, N), BF16), side_shape),
        grid=grid,
        in_specs=[pl.BlockSpec((tm // ns, D), lambda i, j: (i * ns + jnp.minimum(j, ns - 1), 0)), vec, vec, vec,
                  pl.BlockSpec((D, tn), lambda i, j: (0, mm_step(j))), tab, tab, side_in],
        out_specs=(pl.BlockSpec((tm, tn), lambda i, j: (i, mm_step(j))), side_out),
        scratch_shapes=([pltpu.VMEM((tm, D), BF16), pltpu.VMEM((2, tm, LANES), F32)]
                        + [pltpu.VMEM((tm, LANES), F32)] * (SPAN_ACC_BUFS * SPAN_SUB // LANES)),
        compiler_params=_params(("arbitrary", "arbitrary")),
        name="span_proj",
    )(x, g, sh, sc, w, c, s, side.w)


def _with_side(side, grid, in_specs, out_spec, out_shape, operands):
    if side is None:
        return in_specs, out_spec, out_shape, operands
    side_in, side_out, side_shape = side.plan(grid)
    return in_specs + [side_in], (out_spec, side_out), (out_shape, side_shape), operands + (side.w,)


def _ffn_in_kernel(x_ref, g_ref, sh_ref, sc_ref, wg_ref, wu_ref, *rest, side_pair):
    if side_pair is None:
        o_ref, h_ref = rest
    else:
        sw_ref, o_ref, so_ref, h_ref = rest

    @pl.when(pl.program_id(1) == 0)
    def _():
        _norm_modulate_into(x_ref, g_ref, sh_ref, sc_ref, h_ref)

    if side_pair is not None:
        _cast_block(sw_ref, so_ref, side_pair)
    h = h_ref[...]
    gate = jnp.dot(h, wg_ref[...], preferred_element_type=F32)
    up = jnp.dot(h, wu_ref[...], preferred_element_type=F32)
    o_ref[...] = (_silu(gate) * up).astype(o_ref.dtype)


def _ffn_in_proj(x, g, sh, sc, w, side=None, tm=1024, tn=512):
    S, D = x.shape
    F = w.shape[1] // 2
    nj = F // tn
    grid = (S // tm, nj)
    vec = pl.BlockSpec((1, D), lambda i, j: (0, 0))
    in_specs, out_specs, out_shape, operands = _with_side(
        side, grid,
        [pl.BlockSpec((tm, D), lambda i, j: (i, 0)), vec, vec, vec,
         pl.BlockSpec((D, tn), lambda i, j: (0, j)),
         pl.BlockSpec((D, tn), lambda i, j: (0, j + nj))],
        pl.BlockSpec((tm, tn), lambda i, j: (i, j)),
        jax.ShapeDtypeStruct((S, F), BF16),
        (x, g, sh, sc, w, w))
    return pl.pallas_call(
        functools.partial(_ffn_in_kernel, side_pair=None if side is None else side.pair_period),
        out_shape=out_shape,
        grid=grid,
        in_specs=in_specs,
        out_specs=out_specs,
        scratch_shapes=[pltpu.VMEM((tm, D), BF16)],
        compiler_params=_params(("arbitrary", "arbitrary")),
        name="ffn_in_proj",
    )(*operands)


def _out_proj_kernel(a_ref, w_ref, r_ref, gt_ref, *rest, side_pair):
    if side_pair is None:
        (o_ref,) = rest
    else:
        sw_ref, o_ref, so_ref = rest
        _cast_block(sw_ref, so_ref, side_pair)
    acc = jnp.dot(a_ref[...], w_ref[...], preferred_element_type=F32)
    o_ref[...] = r_ref[...] + gt_ref[...] * acc


def _out_proj(a, w, resid, gate, side=None, tm=1024, tn=512):
    S, K = a.shape
    N = w.shape[1]
    grid = (S // tm, N // tn)
    in_specs, out_specs, out_shape, operands = _with_side(
        side, grid,
        [pl.BlockSpec((tm, K), lambda i, j: (i, 0)),
         pl.BlockSpec((K, tn), lambda i, j: (0, j)),
         pl.BlockSpec((tm, tn), lambda i, j: (i, j)),
         pl.BlockSpec((1, tn), lambda i, j: (0, j))],
        pl.BlockSpec((tm, tn), lambda i, j: (i, j)),
        jax.ShapeDtypeStruct((S, N), F32),
        (a, w, resid, gate))
    return pl.pallas_call(
        functools.partial(_out_proj_kernel, side_pair=None if side is None else side.pair_period),
        out_shape=out_shape,
        grid=grid,
        in_specs=in_specs,
        out_specs=out_specs,
        compiler_params=_params(("arbitrary", "arbitrary")),
        name="out_proj",
    )(*operands)


def _retention_kernel(lg_ref, x_ref, sw_ref, o_ref, so_ref, r_ref):
    _cast_block(sw_ref, so_ref)

    @pl.when(pl.program_id(1) == 0)
    def _():
        r_ref[...] = jnp.zeros_like(r_ref)

    C = RET_KERNEL_CHUNK
    k0, v0, g0 = RET_DK, 2 * RET_DK, 2 * RET_DK + RET_DV
    ri = lax.broadcasted_iota(jnp.int32, (C, C), 0)
    ci = lax.broadcasted_iota(jnp.int32, (C, C), 1)
    diff = (ri - ci).astype(F32)
    idx = lax.broadcasted_iota(jnp.int32, (C, 1), 0).astype(F32)
    decays = []
    for hh in range(RET_HEADS_PER_STEP):
        lg = lg_ref[pl.program_id(0) * RET_HEADS_PER_STEP + hh]
        decays.append((jnp.where(diff >= 0, jnp.exp(lg * jnp.maximum(diff, 0.0)), 0.0),
                       jnp.exp(lg * (idx + 1.0)),
                       jnp.exp(lg * (C - 1.0 - idx)),
                       jnp.exp(jnp.full((1, 1), lg * C, F32))))

    def body(c, carry):
        rows = pl.ds(pl.multiple_of(c * C, C), C)
        for hh, (dmask, q_decay, k_decay, chunk_decay) in enumerate(decays):
            x0 = hh * RET_HEAD_COLS
            qc = x_ref[rows, x0:x0 + k0]
            kc = x_ref[rows, x0 + k0:x0 + v0]
            vc = x_ref[rows, x0 + v0:x0 + g0]
            state = r_ref[hh]
            scores = lax.dot_general(qc, kc, (((1,), (1,)), ((), ())), preferred_element_type=F32) * dmask
            o = (jnp.dot(scores.astype(BF16), vc, preferred_element_type=F32)
                 + jnp.dot(qc, state.astype(BF16), preferred_element_type=F32) * q_decay)
            kd = (kc.astype(F32) * k_decay).astype(BF16)
            r_ref[hh] = state * chunk_decay + lax.dot_general(
                kd, vc, (((0,), (0,)), ((), ())), preferred_element_type=F32)
            y = o * lax.rsqrt(jnp.mean(o * o, axis=-1, keepdims=True) + EPS)
            gate = x_ref[rows, x0 + g0:x0 + RET_HEAD_COLS].astype(F32)
            o_ref[rows, hh * RET_DV:(hh + 1) * RET_DV] = (gate * y).astype(o_ref.dtype)
        return carry

    lax.fori_loop(0, x_ref.shape[0] // C, body, 0, unroll=2)


def _retention(qkvg, log_g, side, tb=1024):
    S = qkvg.shape[0]
    hps = RET_HEADS_PER_STEP
    grid = (RET_HEADS // hps, S // tb)
    side_in, side_out, side_shape = side.plan(grid)
    return pl.pallas_call(
        _retention_kernel,
        out_shape=(jax.ShapeDtypeStruct((S, RET_HEADS * RET_DV), BF16), side_shape),
        grid=grid,
        in_specs=[pl.BlockSpec(memory_space=pltpu.SMEM),
                  pl.BlockSpec((tb, hps * RET_HEAD_COLS), lambda h, t: (t, h)), side_in],
        out_specs=(pl.BlockSpec((tb, hps * RET_DV), lambda h, t: (t, h)), side_out),
        scratch_shapes=[pltpu.VMEM((hps, RET_DK, RET_DV), F32)],
        compiler_params=_params(("arbitrary", "arbitrary")),
        name="retention",
    )(log_g, qkvg, side.w)


UNIT_UNROLL = 8


def _attn_unit(q_ref, row, k_own, v_own, k_prev, v_prev, bias):
    rep = Q_HEADS // KV_HEADS
    B = ATTN_BLOCK
    q4 = jnp.concatenate([q_ref[pl.ds(row, B), r * HEAD_DIM:(r + 1) * HEAD_DIM] for r in range(rep)], axis=0)
    k_cat = jnp.concatenate([k_prev, k_own], axis=0)
    v_cat = jnp.concatenate([v_prev, v_own], axis=0)
    v_ext = jnp.concatenate([v_cat, jnp.ones_like(v_cat)], axis=1)
    s = lax.dot_general(q4, k_cat, (((1,), (1,)), ((), ())), preferred_element_type=F32) + bias
    m = jnp.max(s, axis=-1, keepdims=True)
    p = jnp.exp2(s - m).astype(BF16)
    ov = jnp.dot(p, v_ext, preferred_element_type=F32)
    l = ov[:, HEAD_DIM:]
    return ov[:, :HEAD_DIM] / l, m + jnp.log2(l)


def _dilated_attn_kernel(q0_ref, q1_ref, q2_ref, k0_ref, v0_ref, k1_ref, v1_ref, k2_ref, v2_ref,
                         hk0_ref, hv0_ref, hk1_ref, hv1_ref, hk2_ref, hv2_ref, sw_ref,
                         o_ref, so_ref, onat_ref, lnat_ref, bias_ref):
    _cast_block(sw_ref, so_ref)
    i = pl.program_id(0)
    B = ATTN_BLOCK
    rep = Q_HEADS // KV_HEADS
    n_units = SPAN // B
    d1, d2 = DIL_GROUPS[1][1], DIL_GROUPS[2][1]

    @pl.when((i == 0) & (pl.program_id(1) == 0))
    def _():
        ri = lax.broadcasted_iota(jnp.int32, (rep * B, 2 * B), 0) % B
        ci = lax.broadcasted_iota(jnp.int32, (rep * B, 2 * B), 1)
        own = (ci >= B) & (ci - B <= ri)
        prev = (ci < B) & (ci >= ri)
        bias_ref[0] = jnp.where(own | prev, 0.0, MASKED)
        bias_ref[1] = jnp.where(own, 0.0, MASKED)

    def scatter(slot, dst, o, lse):
        for hq in range(rep):
            onat_ref[slot, hq, dst, :] = o[hq * B:(hq + 1) * B, :]
            lnat_ref[slot, hq, dst, :] = lse[hq * B:(hq + 1) * B, :]

    def widest_body(r, carry):
        row = pl.multiple_of(r * B, B)
        rows = pl.ds(row, B)
        bias = bias_ref[(i == 0).astype(jnp.int32)]
        o, lse = _attn_unit(q2_ref, row, k2_ref[rows, :], v2_ref[rows, :], hk2_ref[rows, :], hv2_ref[rows, :], bias)
        scatter(1, pl.ds(r, B, stride=d2), o, lse)
        return carry

    lax.fori_loop(0, n_units, widest_body, 0, unroll=UNIT_UNROLL)

    def middle_body(u, carry):
        s = u // d1
        r = u % d1
        row = pl.multiple_of(u * B, B)
        rows = pl.ds(row, B)
        prow = pl.ds(pl.multiple_of(jnp.maximum(u - d1, 0) * B, B), B)
        hrow = pl.ds(pl.multiple_of(r * B, B), B)
        first = s == 0
        k_prev = jnp.where(first, hk1_ref[hrow, :], k1_ref[prow, :])
        v_prev = jnp.where(first, hv1_ref[hrow, :], v1_ref[prow, :])
        bias = bias_ref[(first & (i == 0)).astype(jnp.int32)]
        o, lse = _attn_unit(q1_ref, row, k1_ref[rows, :], v1_ref[rows, :], k_prev, v_prev, bias)
        scatter(0, pl.ds(s * (B * d1) + r, B, stride=d1), o, lse)
        return carry

    lax.fori_loop(0, n_units, middle_body, 0, unroll=UNIT_UNROLL)

    def dense_body(b, carry):
        row = pl.multiple_of(b * B, B)
        rows = pl.ds(row, B)
        prow = pl.ds(pl.multiple_of(jnp.maximum(b - 1, 0) * B, B), B)
        first = b == 0
        k_prev = jnp.where(first, hk0_ref[...], k0_ref[prow, :])
        v_prev = jnp.where(first, hv0_ref[...], v0_ref[prow, :])
        bias = bias_ref[(first & (i == 0)).astype(jnp.int32)]
        o0, l0 = _attn_unit(q0_ref, row, k0_ref[rows, :], v0_ref[rows, :], k_prev, v_prev, bias)
        for hq in range(rep):
            a0 = o0[hq * B:(hq + 1) * B, :]
            e0 = l0[hq * B:(hq + 1) * B, :]
            e1 = lnat_ref[0, hq, rows, :]
            e2 = lnat_ref[1, hq, rows, :]
            mx = jnp.maximum(jnp.maximum(e0, e1), e2)
            w0, w1, w2 = jnp.exp2(e0 - mx), jnp.exp2(e1 - mx), jnp.exp2(e2 - mx)
            merged = (w0 * a0 + w1 * onat_ref[0, hq, rows, :] + w2 * onat_ref[1, hq, rows, :]) / (w0 + w1 + w2)
            o_ref[rows, hq * HEAD_DIM:(hq + 1) * HEAD_DIM] = merged.astype(o_ref.dtype)
        return carry

    lax.fori_loop(0, n_units, dense_body, 0, unroll=UNIT_UNROLL)


def _dilated_attention(q_all, kv_all, side):
    S = q_all.shape[0]
    rep = Q_HEADS // KV_HEADS
    qw = rep * HEAD_DIM
    per_group = 2 * KV_HEADS
    B = ATTN_BLOCK

    def q_spec(g):
        return pl.BlockSpec((SPAN, qw), lambda i, hd: (i, g * KV_HEADS + hd))

    def kv_spec(g, is_v):
        return pl.BlockSpec((SPAN, HEAD_DIM), lambda i, hd: (i, g * per_group + is_v * KV_HEADS + hd))

    def halo_spec(g, is_v):
        rows = B * DIL_GROUPS[g][1]
        n = SPAN // rows
        return pl.BlockSpec((rows, HEAD_DIM),
                            lambda i, hd: (jnp.maximum(i * n - 1, 0), g * per_group + is_v * KV_HEADS + hd))

    groups = range(len(DIL_GROUPS))
    in_specs = ([q_spec(g) for g in groups]
                + [kv_spec(g, v) for g in groups for v in (0, 1)]
                + [halo_spec(g, v) for g in groups for v in (0, 1)])
    grid = (S // SPAN, KV_HEADS)
    side_in, side_out, side_shape = side.plan(grid)
    return pl.pallas_call(
        _dilated_attn_kernel,
        out_shape=(jax.ShapeDtypeStruct((S, Q_HEADS * HEAD_DIM), BF16), side_shape),
        grid=grid,
        in_specs=in_specs + [side_in],
        out_specs=(pl.BlockSpec((SPAN, qw), lambda i, hd: (i, hd)), side_out),
        scratch_shapes=[pltpu.VMEM((2, rep, SPAN, LANES), F32),
                        pltpu.VMEM((2, rep, SPAN, LANES), F32),
                        pltpu.VMEM((2, rep * B, 2 * B), F32)],
        compiler_params=_params(("arbitrary", "arbitrary")),
        name="dilated_attn",
    )(*([q_all] * 3 + [kv_all] * 12 + [side.w]))


def _final_norm_kernel(x_ref, g_ref, o_ref):
    x = x_ref[...]
    ms = jnp.mean(x * x, axis=-1, keepdims=True)
    o_ref[...] = x * lax.rsqrt(ms + EPS) * g_ref[...]


def _final_norm(x, g, tm=512):
    S, D = x.shape
    return pl.pallas_call(
        _final_norm_kernel,
        out_shape=jax.ShapeDtypeStruct((S, D), F32),
        grid=(S // tm,),
        in_specs=[pl.BlockSpec((tm, D), lambda i: (i, 0)), pl.BlockSpec((1, D), lambda i: (0, 0))],
        out_specs=pl.BlockSpec((tm, D), lambda i: (i, 0)),
        compiler_params=_params(("parallel",)),
        name="final_norm",
    )(x, g)


def kernel(x, c, positions, ada_w, ada_b, norm_g, ffn_w_in, ffn_w_out, ret_w_in, ret_w_out, kv_norm_g, kv_ada_w, kv_ada_b, kv_w, attn_w_q, attn_w_out, final_norm_g):
    B, S, D = x.shape
    assert B == 1 and S % SPAN == 0
    assert all(w // d == ATTN_BLOCK for w, d in DIL_GROUPS) and DIL_GROUPS[0][1] == 1
    xs = x.reshape(S, D)

    mod = _ada_matvec(c, ada_w, ada_b[:, None, :])
    kv_mod = _ada_matvec(c, kv_ada_w[None], kv_ada_b[None, None, :])[0]

    def mod_vec(layer, idx):
        return mod[layer, :, idx * D:(idx + 1) * D]

    inv_freq_ret = 1.0 / (RET_THETA ** jnp.linspace(0.0, 1.0, RET_DK // 2, dtype=F32))
    inv_freq_attn = ROPE_THETA ** (-jnp.arange(0, ROT_DIM, 2, dtype=F32) / ROT_DIM)
    gap = jnp.zeros((ROT_PARTNER - ROT_DIM // 2,), F32)
    inv_freq_attn = jnp.concatenate([inv_freq_attn, gap, inv_freq_attn, gap])
    pos_col = positions.reshape(S, 1).astype(F32)
    cos_r, sin_r, cos_a, sin_a = _rope_tables(pos_col, inv_freq_ret[None, :], inv_freq_attn[None, :])

    w_ret_in = _prep_weight(ret_w_in, bw=RET_DK, src_block=_ret_head_src_block)
    qkvg, w_ret_out = _ret_in_proj(xs, norm_g[0, 0][None], mod_vec(0, 0), mod_vec(0, 1), w_ret_in, cos_r, sin_r,
                                   _SideCast(ret_w_out))
    log_g = jnp.log1p(-(2.0 ** (-5.0 - jnp.arange(RET_HEADS, dtype=F32))))
    y, w_ffn_in0 = _retention(qkvg, log_g, _SideCast(ffn_w_in, 0))
    xs, w_kv = _out_proj(y, w_ret_out, xs, mod_vec(0, 2), _SideCast(kv_w[None], pair_period=2))
    hid, w_ffn_out0 = _ffn_in_proj(xs, norm_g[0, 1][None], mod_vec(0, 3), mod_vec(0, 4), w_ffn_in0,
                                   _SideCast(ffn_w_out, 0))
    xs, w_q = _out_proj(hid, w_ffn_out0, xs, mod_vec(0, 5), _SideCast(attn_w_q, pair_period=1))

    kv_all, w_attn_out = _span_proj(xs, kv_norm_g[None], kv_mod[:, :D], kv_mod[:, D:], w_kv, cos_a, sin_a,
                                    _SideCast(attn_w_out), out_scale=1.0, rope_cols=PAIR_TILE)

    q_all, w_ffn_in1 = _span_proj(xs, norm_g[1, 0][None], mod_vec(1, 0), mod_vec(1, 1), w_q, cos_a, sin_a,
                                  _SideCast(ffn_w_in, 1), out_scale=HEAD_DIM ** -0.5 * LOG2E, rope_cols=SPAN_TN)
    attn, w_ffn_out1 = _dilated_attention(q_all, kv_all, _SideCast(ffn_w_out, 1))
    xs = _out_proj(attn, w_attn_out, xs, mod_vec(1, 2))
    hid = _ffn_in_proj(xs, norm_g[1, 1][None], mod_vec(1, 3), mod_vec(1, 4), w_ffn_in1)
    xs = _out_proj(hid, w_ffn_out1, xs, mod_vec(1, 5))

    return _final_norm(xs, final_norm_g[None]).reshape(B, S, D)
```

```python
import functools

import jax
import jax.numpy as jnp
from jax import lax
from jax.experimental import pallas as pl
from jax.experimental.pallas import tpu as pltpu

F32 = jnp.float32
BF16 = jnp.bfloat16

EPS = 1e-6
RET_HEADS = 8
RET_DK = 256
RET_DV = 512
RET_CHUNK = 128
RET_KERNEL_CHUNK = 256
RET_HEADS_PER_STEP = 2
RET_THETA = 10000.0
DIL_GROUPS = ((128, 1), (512, 4), (2048, 16))
HEAD_DIM = 128
Q_HEADS = 16
KV_HEADS = 4
ROT_DIM = 32
ROPE_THETA = 500000.0
ATTN_BLOCK = 128
N_MOD = 6

LANES = 128
VMEM_LIMIT = 56 * 1024 * 1024
SPAN = ATTN_BLOCK * max(d for _, d in DIL_GROUPS)
ROT_PARTNER = LANES // 2
MASKED = -1e30
LOG2E = 1.4426950408889634


def _params(sem, vmem=VMEM_LIMIT):
    return pltpu.CompilerParams(dimension_semantics=sem, vmem_limit_bytes=vmem)


def _silu(x):
    return x / (1.0 + jnp.exp(-x))


def _matvec_kernel(c_ref, w_ref, b_ref, o_ref):
    c = c_ref[...]
    ca = jnp.broadcast_to(_silu(c), (8, c.shape[1])).astype(BF16)
    acc = jnp.dot(ca, w_ref[...].astype(BF16), preferred_element_type=F32)
    o_ref[...] = acc[0:1, :] + b_ref[...]


def _ada_matvec(c, w, b, tn=1024):
    L, D, N = w.shape
    return pl.pallas_call(
        _matvec_kernel,
        out_shape=jax.ShapeDtypeStruct((L, 1, N), F32),
        grid=(L, N // tn),
        in_specs=[
            pl.BlockSpec((1, D), lambda l, j: (0, 0)),
            pl.BlockSpec((None, D, tn), lambda l, j: (l, 0, j)),
            pl.BlockSpec((None, 1, tn), lambda l, j: (l, 0, j)),
        ],
        out_specs=pl.BlockSpec((None, 1, tn), lambda l, j: (l, 0, j)),
        compiler_params=_params(("parallel", "parallel")),
        name="ada_matvec",
    )(c, w, b)


def _rope_block(pos_ref, f_ref, cos_ref, sin_ref, paired):
    ang = pos_ref[...] * f_ref[...]
    cos_ref[...] = jnp.cos(ang)
    sin = jnp.sin(ang)
    if paired:
        lane = lax.broadcasted_iota(jnp.int32, ang.shape, 1)
        sin = jnp.where(lane < ROT_PARTNER, -sin, sin)
    sin_ref[...] = sin


def _rope_table_kernel(pos_ref, fr_ref, fa_ref, cr_ref, sr_ref, ca_ref, sa_ref):
    _rope_block(pos_ref, fr_ref, cr_ref, sr_ref, paired=False)
    _rope_block(pos_ref, fa_ref, ca_ref, sa_ref, paired=True)


def _rope_tables(pos_col, inv_freq_ret, inv_freq_attn, tb=1024):
    S = pos_col.shape[0]
    tab = jax.ShapeDtypeStruct((S, LANES), F32)
    row = pl.BlockSpec((tb, LANES), lambda i: (i, 0))
    vec = pl.BlockSpec((1, LANES), lambda i: (0, 0))
    return pl.pallas_call(
        _rope_table_kernel,
        out_shape=(tab,) * 4,
        grid=(S // tb,),
        in_specs=[pl.BlockSpec((tb, 1), lambda i: (i, 0)), vec, vec],
        out_specs=(row,) * 4,
        compiler_params=_params(("parallel",)),
        name="rope_tables",
    )(pos_col, inv_freq_ret, inv_freq_attn)


PAIR_TILE = KV_HEADS * HEAD_DIM


def _cast_block(w_ref, o_ref, pair_period=0):
    if pair_period == 0:
        o_ref[...] = w_ref[...].astype(o_ref.dtype)
        return
    half = ROT_DIM // 2
    shift = ROT_PARTNER - half
    for ct in range(w_ref.shape[1] // PAIR_TILE):
        if ct % pair_period != 0:
            cols = slice(ct * PAIR_TILE, (ct + 1) * PAIR_TILE)
            o_ref[:, cols] = w_ref[:, cols].astype(o_ref.dtype)
            continue
        for hh in range(PAIR_TILE // HEAD_DIM):
            cols = slice(ct * PAIR_TILE + hh * HEAD_DIM, ct * PAIR_TILE + (hh + 1) * HEAD_DIM)
            x = w_ref[:, cols]
            lane = lax.broadcasted_iota(jnp.int32, x.shape, 1)
            from_hi = (lane >= half) & (lane < ROT_DIM)
            from_lo = (lane >= ROT_PARTNER) & (lane < ROT_PARTNER + half)
            y = jnp.where(from_hi, pltpu.roll(x, HEAD_DIM - shift, 1), jnp.where(from_lo, pltpu.roll(x, shift, 1), x))
            o_ref[:, cols] = y.astype(o_ref.dtype)


def _prep_weight_kernel(w_ref, o_ref):
    _cast_block(w_ref, o_ref)


def _prep_weight(w, layer=0, *, bw=512, src_block=None):
    _, K, N = w.shape
    src = src_block if src_block is not None else (lambda j: j)
    return pl.pallas_call(
        _prep_weight_kernel,
        out_shape=jax.ShapeDtypeStruct((K, N), BF16),
        grid=(N // bw,),
        in_specs=[pl.BlockSpec((None, K, bw), lambda j: (layer, 0, src(j)))],
        out_specs=pl.BlockSpec((K, bw), lambda j: (0, j)),
        compiler_params=_params(("parallel",)),
        name="prep_weight",
    )(w)


class _SideCast:
    def __init__(self, w, layer=0, pair_period=0):
        self.w, self.layer, self.pair_period = w, layer, pair_period

    def plan(self, grid):
        _, K, N = self.w.shape
        n_steps, nj, layer = grid[0] * grid[1], grid[1], self.layer
        bf16_rows = 16
        n_chunks = max(n for n in range(1, n_steps + 1) if K % (n * bf16_rows) == 0)
        rows = K // n_chunks
        chunk = lambda i, j: jnp.minimum(i * nj + j, n_chunks - 1)
        return (pl.BlockSpec((None, rows, N), lambda i, j: (layer, chunk(i, j), 0)),
                pl.BlockSpec((rows, N), lambda i, j: (chunk(i, j), 0)),
                jax.ShapeDtypeStruct((K, N), BF16))


NORM_ROWS = 128


def _norm_modulate_into(x_ref, g_ref, sh_ref, sc_ref, h_ref, row0=0):
    a = g_ref[...] * (1.0 + sc_ref[...])
    b = sh_ref[...]

    def body(c, carry):
        off = pl.multiple_of(c * NORM_ROWS, NORM_ROWS)
        xc = x_ref[pl.ds(off, NORM_ROWS), :]
        ms = jnp.mean(xc * xc, axis=-1, keepdims=True)
        h_ref[pl.ds(pl.multiple_of(row0 + off, NORM_ROWS), NORM_ROWS), :] = (
            xc * lax.rsqrt(ms + EPS) * a + b).astype(BF16)
        return carry

    lax.fori_loop(0, x_ref.shape[0] // NORM_ROWS, body, 0)


RET_HEAD_COLS = 2 * RET_DK + 2 * RET_DV


def _ret_head_src_block(ob):
    per_head = RET_HEAD_COLS // RET_DK
    vblk = RET_DV // RET_DK
    h, p = ob // per_head, ob % per_head
    q_src = h
    k_src = RET_HEADS + h
    v_src = 2 * RET_HEADS + vblk * h + (p - 2)
    g_src = (2 + vblk) * RET_HEADS + vblk * h + (p - 2 - vblk)
    return jnp.where(p == 0, q_src, jnp.where(p == 1, k_src, jnp.where(p < 2 + vblk, v_src, g_src)))


def _ret_in_kernel(x_ref, g_ref, sh_ref, sc_ref, w_ref, cos_ref, sin_ref, sw_ref, o_ref, so_ref, h_ref):
    @pl.when(pl.program_id(1) == 0)
    def _():
        _norm_modulate_into(x_ref, g_ref, sh_ref, sc_ref, h_ref)

    _cast_block(sw_ref, so_ref)
    h = h_ref[...]
    half = RET_DK // 2
    v0, g0 = 2 * RET_DK, 2 * RET_DK + RET_DV

    qk = jnp.dot(h, w_ref[:, :v0], preferred_element_type=F32)
    for lo, scale in ((0, 1.0), (RET_DK, RET_DK ** -0.5)):
        cos = cos_ref[...] * scale
        sin = sin_ref[...] * scale
        a1 = qk[:, lo:lo + half]
        a2 = qk[:, lo + half:lo + RET_DK]
        o_ref[:, lo:lo + half] = (a1 * cos - a2 * sin).astype(o_ref.dtype)
        o_ref[:, lo + half:lo + RET_DK] = (a2 * cos + a1 * sin).astype(o_ref.dtype)

    gate = jnp.dot(h, w_ref[:, g0:], preferred_element_type=F32)
    o_ref[:, g0:] = _silu(gate).astype(o_ref.dtype)
    v = jnp.dot(h, w_ref[:, v0:g0], preferred_element_type=F32)
    o_ref[:, v0:g0] = v.astype(o_ref.dtype)


def _ret_in_proj(x, g, sh, sc, w, cos, sin, side, tm=1024):
    S, D = x.shape
    N = w.shape[1]
    tn = RET_HEAD_COLS
    grid = (S // tm, N // tn)
    vec = pl.BlockSpec((1, D), lambda i, j: (0, 0))
    tab = pl.BlockSpec((tm, LANES), lambda i, j: (i, 0))
    side_in, side_out, side_shape = side.plan(grid)
    return pl.pallas_call(
        _ret_in_kernel,
        out_shape=(jax.ShapeDtypeStruct((S, N), BF16), side_shape),
        grid=grid,
        in_specs=[pl.BlockSpec((tm, D), lambda i, j: (i, 0)), vec, vec, vec,
                  pl.BlockSpec((D, tn), lambda i, j: (0, j)), tab, tab, side_in],
        out_specs=(pl.BlockSpec((tm, tn), lambda i, j: (i, j)), side_out),
        scratch_shapes=[pltpu.VMEM((tm, D), BF16)],
        compiler_params=_params(("arbitrary", "arbitrary")),
        name="ret_in_proj",
    )(x, g, sh, sc, w, cos, sin, side.w)


SPAN_NORM_STEPS = 4
SPAN_SUB = 256
SPAN_TN = 1024
SPAN_ACC_BUFS = 2


def _span_chunks(d):
    B = ATTN_BLOCK
    out = []
    for s in range(SPAN // (B * d)):
        for r in range(d):
            u = s * d + r
            out.append((pl.ds(s * B * d + r, B, stride=d) if d > 1 else pl.ds(u * B, B), pl.ds(u * B, B)))
    return out


CHEAP_STRIDE = 4


def _presplit_rows(src_ref, dst_ref):
    q = src_ref.shape[0] // CHEAP_STRIDE
    for r in range(CHEAP_STRIDE):
        dst_ref[r * q:(r + 1) * q, :] = src_ref[pl.ds(r, q, stride=CHEAP_STRIDE), :]


def _span_chunks_presplit(d):
    B = ATTN_BLOCK
    assert d % CHEAP_STRIDE == 0 and B * d == SPAN
    q = SPAN // CHEAP_STRIDE
    return [(pl.ds((r % CHEAP_STRIDE) * q + r // CHEAP_STRIDE, B, stride=d // CHEAP_STRIDE), pl.ds(r * B, B))
            for r in range(d)]


def _span_proj_kernel(x_ref, g_ref, sh_ref, sc_ref, w_ref, c_ref, s_ref, sw_ref, o_ref, so_ref,
                      h_ref, tab_ref, *acc_refs, out_scale, steps_per_group, rope_cols):
    j = pl.program_id(1)
    jm = j - SPAN_NORM_STEPS

    @pl.when(jm < 0)
    def _():
        _cast_block(sw_ref, so_ref)
        _norm_modulate_into(x_ref, g_ref, sh_ref, sc_ref, h_ref, row0=j * (SPAN // SPAN_NORM_STEPS))

    def run(d):
        chunks = _span_chunks(d)
        data_chunks = _span_chunks_presplit(d) if d > CHEAP_STRIDE else chunks

        @pl.when(jm % steps_per_group == 0)
        def _():
            for src, dst in chunks:
                tab_ref[0, dst, :] = c_ref[src, :] * out_scale
                tab_ref[1, dst, :] = s_ref[src, :] * out_scale

        _cast_block(sw_ref, so_ref)
        n_slab = SPAN_SUB // LANES
        for t in range(w_ref.shape[1] // SPAN_SUB):
            col0 = t * SPAN_SUB
            acc = jnp.dot(h_ref[...], w_ref[:, col0:col0 + SPAN_SUB], preferred_element_type=F32)
            buf = (t % SPAN_ACC_BUFS) * n_slab
            slabs = acc_refs[buf:buf + n_slab]
            for c in range(n_slab):
                slabs[c][...] = acc[:, c * LANES:(c + 1) * LANES]
            if d > CHEAP_STRIDE:
                split = acc_refs[SPAN_ACC_BUFS * n_slab + buf:SPAN_ACC_BUFS * n_slab + buf + n_slab]
                for c in range(n_slab):
                    _presplit_rows(slabs[c], split[c])
                slabs = split
            rope = col0 < rope_cols
            for src, dst in data_chunks:
                if rope:
                    ct = tab_ref[0, dst, :]
                    st = tab_ref[1, dst, :]
                for c in range(n_slab):
                    xr = slabs[c][src, :]
                    y = xr * ct + pltpu.roll(xr, ROT_PARTNER, 1) * st if rope else xr
                    o_ref[dst, col0 + c * LANES:col0 + (c + 1) * LANES] = y.astype(o_ref.dtype)

    for gi, (_, d) in enumerate(DIL_GROUPS):
        pl.when((jm >= 0) & (jm // steps_per_group == gi))(functools.partial(run, d))


def _span_proj(x, g, sh, sc, w, c, s, side, *, out_scale, rope_cols):
    S, D = x.shape
    N = w.shape[1]
    tm, tn, ns = SPAN, SPAN_TN, SPAN_NORM_STEPS
    n_mm = N // tn
    grid = (S // tm, ns + n_mm)
    vec = pl.BlockSpec((1, D), lambda i, j: (0, 0))
    tab = pl.BlockSpec((tm, LANES), lambda i, j: (i, 0))
    mm_step = lambda j: jnp.maximum(j - ns, 0)
    side_in, side_out, side_shape = side.plan(grid)
    return pl.pallas_call(
        functools.partial(_span_proj_kernel, out_scale=out_scale,
                          steps_per_group=n_mm // len(DIL_GROUPS), rope_cols=rope_cols),
        out_shape=(jax.ShapeDtypeStruct((S, N), BF16), side_shape),
        grid=grid,
        in_specs=[pl.BlockSpec((tm // ns, D), lambda i, j: (i * ns + jnp.minimum(j, ns - 1), 0)), vec, vec, vec,
                  pl.BlockSpec((D, tn), lambda i, j: (0, mm_step(j))), tab, tab, side_in],
        out_specs=(pl.BlockSpec((tm, tn), lambda i, j: (i, mm_step(j))), side_out),
        scratch_shapes=([pltpu.VMEM((tm, D), BF16), pltpu.VMEM((2, tm, LANES), F32)]
                        + [pltpu.VMEM((tm, LANES), F32)] * (2 * SPAN_ACC_BUFS * SPAN_SUB // LANES)),
        compiler_params=_params(("arbitrary", "arbitrary")),
        name="span_proj",
    )(x, g, sh, sc, w, c, s, side.w)


def _with_side(side, grid, in_specs, out_spec, out_shape, operands):
    if side is None:
        return in_specs, out_spec, out_shape, operands
    side_in, side_out, side_shape = side.plan(grid)
    return in_specs + [side_in], (out_spec, side_out), (out_shape, side_shape), operands + (side.w,)


def _ffn_in_kernel(x_ref, g_ref, sh_ref, sc_ref, wg_ref, wu_ref, *rest, side_pair):
    if side_pair is None:
        o_ref, h_ref = rest
    else:
        sw_ref, o_ref, so_ref, h_ref = rest

    @pl.when(pl.program_id(1) == 0)
    def _():
        _norm_modulate_into(x_ref, g_ref, sh_ref, sc_ref, h_ref)

    if side_pair is not None:
        _cast_block(sw_ref, so_ref, side_pair)
    h = h_ref[...]
    gate = jnp.dot(h, wg_ref[...], preferred_element_type=F32)
    up = jnp.dot(h, wu_ref[...], preferred_element_type=F32)
    o_ref[...] = (_silu(gate) * up).astype(o_ref.dtype)


def _ffn_in_proj(x, g, sh, sc, w, side=None, tm=1024, tn=512):
    S, D = x.shape
    F = w.shape[1] // 2
    nj = F // tn
    grid = (S // tm, nj)
    vec = pl.BlockSpec((1, D), lambda i, j: (0, 0))
    in_specs, out_specs, out_shape, operands = _with_side(
        side, grid,
        [pl.BlockSpec((tm, D), lambda i, j: (i, 0)), vec, vec, vec,
         pl.BlockSpec((D, tn), lambda i, j: (0, j)),
         pl.BlockSpec((D, tn), lambda i, j: (0, j + nj))],
        pl.BlockSpec((tm, tn), lambda i, j: (i, j)),
        jax.ShapeDtypeStruct((S, F), BF16),
        (x, g, sh, sc, w, w))
    return pl.pallas_call(
        functools.partial(_ffn_in_kernel, side_pair=None if side is None else side.pair_period),
        out_shape=out_shape,
        grid=grid,
        in_specs=in_specs,
        out_specs=out_specs,
        scratch_shapes=[pltpu.VMEM((tm, D), BF16)],
        compiler_params=_params(("arbitrary", "arbitrary")),
        name="ffn_in_proj",
    )(*operands)


def _out_proj_kernel(a_ref, w_ref, r_ref, gt_ref, *rest, side_pair):
    if side_pair is None:
        (o_ref,) = rest
    else:
        sw_ref, o_ref, so_ref = rest
        _cast_block(sw_ref, so_ref, side_pair)
    acc = jnp.dot(a_ref[...], w_ref[...], preferred_element_type=F32)
    o_ref[...] = r_ref[...] + gt_ref[...] * acc


def _out_proj(a, w, resid, gate, side=None, tm=1024, tn=512):
    S, K = a.shape
    N = w.shape[1]
    grid = (S // tm, N // tn)
    in_specs, out_specs, out_shape, operands = _with_side(
        side, grid,
        [pl.BlockSpec((tm, K), lambda i, j: (i, 0)),
         pl.BlockSpec((K, tn), lambda i, j: (0, j)),
         pl.BlockSpec((tm, tn), lambda i, j: (i, j)),
         pl.BlockSpec((1, tn), lambda i, j: (0, j))],
        pl.BlockSpec((tm, tn), lambda i, j: (i, j)),
        jax.ShapeDtypeStruct((S, N), F32),
        (a, w, resid, gate))
    return pl.pallas_call(
        functools.partial(_out_proj_kernel, side_pair=None if side is None else side.pair_period),
        out_shape=out_shape,
        grid=grid,
        in_specs=in_specs,
        out_specs=out_specs,
        compiler_params=_params(("arbitrary", "arbitrary")),
        name="out_proj",
    )(*operands)


def _retention_kernel(lg_ref, x_ref, sw_ref, o_ref, so_ref, r_ref):
    _cast_block(sw_ref, so_ref)

    @pl.when(pl.program_id(1) == 0)
    def _():
        r_ref[...] = jnp.zeros_like(r_ref)

    C = RET_KERNEL_CHUNK
    k0, v0, g0 = RET_DK, 2 * RET_DK, 2 * RET_DK + RET_DV
    ri = lax.broadcasted_iota(jnp.int32, (C, C), 0)
    ci = lax.broadcasted_iota(jnp.int32, (C, C), 1)
    diff = (ri - ci).astype(F32)
    idx = lax.broadcasted_iota(jnp.int32, (C, 1), 0).astype(F32)
    decays = []
    for hh in range(RET_HEADS_PER_STEP):
        lg = lg_ref[pl.program_id(0) * RET_HEADS_PER_STEP + hh]
        decays.append((jnp.where(diff >= 0, jnp.exp(lg * jnp.maximum(diff, 0.0)), 0.0),
                       jnp.exp(lg * (idx + 1.0)),
                       jnp.exp(lg * (C - 1.0 - idx)),
                       jnp.exp(jnp.full((1, 1), lg * C, F32))))

    def body(c, carry):
        rows = pl.ds(pl.multiple_of(c * C, C), C)
        for hh, (dmask, q_decay, k_decay, chunk_decay) in enumerate(decays):
            x0 = hh * RET_HEAD_COLS
            qc = x_ref[rows, x0:x0 + k0]
            kc = x_ref[rows, x0 + k0:x0 + v0]
            vc = x_ref[rows, x0 + v0:x0 + g0]
            state = r_ref[hh]
            scores = lax.dot_general(qc, kc, (((1,), (1,)), ((), ())), preferred_element_type=F32) * dmask
            o = (jnp.dot(scores.astype(BF16), vc, preferred_element_type=F32)
                 + jnp.dot(qc, state.astype(BF16), preferred_element_type=F32) * q_decay)
            kd = (kc.astype(F32) * k_decay).astype(BF16)
            r_ref[hh] = state * chunk_decay + lax.dot_general(
                kd, vc, (((0,), (0,)), ((), ())), preferred_element_type=F32)
            y = o * lax.rsqrt(jnp.mean(o * o, axis=-1, keepdims=True) + EPS)
            gate = x_ref[rows, x0 + g0:x0 + RET_HEAD_COLS].astype(F32)
            o_ref[rows, hh * RET_DV:(hh + 1) * RET_DV] = (gate * y).astype(o_ref.dtype)
        return carry

    lax.fori_loop(0, x_ref.shape[0] // C, body, 0, unroll=2)


def _retention(qkvg, log_g, side, tb=2048):
    S = qkvg.shape[0]
    hps = RET_HEADS_PER_STEP
    grid = (RET_HEADS // hps, S // tb)
    side_in, side_out, side_shape = side.plan(grid)
    return pl.pallas_call(
        _retention_kernel,
        out_shape=(jax.ShapeDtypeStruct((S, RET_HEADS * RET_DV), BF16), side_shape),
        grid=grid,
        in_specs=[pl.BlockSpec(memory_space=pltpu.SMEM),
                  pl.BlockSpec((tb, hps * RET_HEAD_COLS), lambda h, t: (t, h)), side_in],
        out_specs=(pl.BlockSpec((tb, hps * RET_DV), lambda h, t: (t, h)), side_out),
        scratch_shapes=[pltpu.VMEM((hps, RET_DK, RET_DV), F32)],
        compiler_params=_params(("arbitrary", "arbitrary")),
        name="retention",
    )(log_g, qkvg, side.w)


UNIT_UNROLL = 8


def _attn_unit(q_ref, row, k_own, v_own, k_prev, v_prev, bias):
    rep = Q_HEADS // KV_HEADS
    B = ATTN_BLOCK
    q4 = jnp.concatenate([q_ref[pl.ds(row, B), r * HEAD_DIM:(r + 1) * HEAD_DIM] for r in range(rep)], axis=0)
    k_cat = jnp.concatenate([k_prev, k_own], axis=0)
    v_cat = jnp.concatenate([v_prev, v_own], axis=0)
    v_ext = jnp.concatenate([v_cat, jnp.ones_like(v_cat)], axis=1)
    s = lax.dot_general(q4, k_cat, (((1,), (1,)), ((), ())), preferred_element_type=F32) + bias
    m = jnp.max(s, axis=-1, keepdims=True)
    p = jnp.exp2(s - m).astype(BF16)
    ov = jnp.dot(p, v_ext, preferred_element_type=F32)
    l = ov[:, HEAD_DIM:]
    return ov[:, :HEAD_DIM] / l, m + jnp.log2(l)


def _dilated_attn_kernel(q0_ref, q1_ref, q2_ref, k0_ref, v0_ref, k1_ref, v1_ref, k2_ref, v2_ref,
                         hk0_ref, hv0_ref, hk1_ref, hv1_ref, hk2_ref, hv2_ref, sw_ref,
                         o_ref, so_ref, onat_ref, lnat_ref, bias_ref):
    _cast_block(sw_ref, so_ref)
    i = pl.program_id(0)
    B = ATTN_BLOCK
    rep = Q_HEADS // KV_HEADS
    n_units = SPAN // B
    d1, d2 = DIL_GROUPS[1][1], DIL_GROUPS[2][1]

    @pl.when((i == 0) & (pl.program_id(1) == 0))
    def _():
        ri = lax.broadcasted_iota(jnp.int32, (rep * B, 2 * B), 0) % B
        ci = lax.broadcasted_iota(jnp.int32, (rep * B, 2 * B), 1)
        own = (ci >= B) & (ci - B <= ri)
        prev = (ci < B) & (ci >= ri)
        bias_ref[0] = jnp.where(own | prev, 0.0, MASKED)
        bias_ref[1] = jnp.where(own, 0.0, MASKED)

    def scatter(slot, dst, o, lse):
        for hq in range(rep):
            onat_ref[slot, hq, dst, :] = o[hq * B:(hq + 1) * B, :]
            lnat_ref[slot, hq, dst, :] = lse[hq * B:(hq + 1) * B, :]

    def widest_body(r, carry):
        row = pl.multiple_of(r * B, B)
        rows = pl.ds(row, B)
        bias = bias_ref[(i == 0).astype(jnp.int32)]
        o, lse = _attn_unit(q2_ref, row, k2_ref[rows, :], v2_ref[rows, :], hk2_ref[rows, :], hv2_ref[rows, :], bias)
        scatter(1, pl.ds(r, B, stride=d2), o, lse)
        return carry

    lax.fori_loop(0, n_units, widest_body, 0, unroll=UNIT_UNROLL)

    def middle_body(u, carry):
        s = u // d1
        r = u % d1
        row = pl.multiple_of(u * B, B)
        rows = pl.ds(row, B)
        prow = pl.ds(pl.multiple_of(jnp.maximum(u - d1, 0) * B, B), B)
        hrow = pl.ds(pl.multiple_of(r * B, B), B)
        first = s == 0
        k_prev = jnp.where(first, hk1_ref[hrow, :], k1_ref[prow, :])
        v_prev = jnp.where(first, hv1_ref[hrow, :], v1_ref[prow, :])
        bias = bias_ref[(first & (i == 0)).astype(jnp.int32)]
        o, lse = _attn_unit(q1_ref, row, k1_ref[rows, :], v1_ref[rows, :], k_prev, v_prev, bias)
        scatter(0, pl.ds(s * (B * d1) + r, B, stride=d1), o, lse)
        return carry

    lax.fori_loop(0, n_units, middle_body, 0, unroll=UNIT_UNROLL)

    def dense_body(b, carry):
        row = pl.multiple_of(b * B, B)
        rows = pl.ds(row, B)
        prow = pl.ds(pl.multiple_of(jnp.maximum(b - 1, 0) * B, B), B)
        first = b == 0
        k_prev = jnp.where(first, hk0_ref[...], k0_ref[prow, :])
        v_prev = jnp.where(first, hv0_ref[...], v0_ref[prow, :])
        bias = bias_ref[(first & (i == 0)).astype(jnp.int32)]
        o0, l0 = _attn_unit(q0_ref, row, k0_ref[rows, :], v0_ref[rows, :], k_prev, v_prev, bias)
        for hq in range(rep):
            a0 = o0[hq * B:(hq + 1) * B, :]
            e0 = l0[hq * B:(hq + 1) * B, :]
            e1 = lnat_ref[0, hq, rows, :]
            e2 = lnat_ref[1, hq, rows, :]
            mx = jnp.maximum(jnp.maximum(e0, e1), e2)
            w0, w1, w2 = jnp.exp2(e0 - mx), jnp.exp2(e1 - mx), jnp.exp2(e2 - mx)
            merged = (w0 * a0 + w1 * onat_ref[0, hq, rows, :] + w2 * onat_ref[1, hq, rows, :]) / (w0 + w1 + w2)
            o_ref[rows, hq * HEAD_DIM:(hq + 1) * HEAD_DIM] = merged.astype(o_ref.dtype)
        return carry

    lax.fori_loop(0, n_units, dense_body, 0, unroll=UNIT_UNROLL)


def _dilated_attention(q_all, kv_all, side):
    S = q_all.shape[0]
    rep = Q_HEADS // KV_HEADS
    qw = rep * HEAD_DIM
    per_group = 2 * KV_HEADS
    B = ATTN_BLOCK

    def q_spec(g):
        return pl.BlockSpec((SPAN, qw), lambda i, hd: (i, g * KV_HEADS + hd))

    def kv_spec(g, is_v):
        return pl.BlockSpec((SPAN, HEAD_DIM), lambda i, hd: (i, g * per_group + is_v * KV_HEADS + hd))

    def halo_spec(g, is_v):
        rows = B * DIL_GROUPS[g][1]
        n = SPAN // rows
        return pl.BlockSpec((rows, HEAD_DIM),
                            lambda i, hd: (jnp.maximum(i * n - 1, 0), g * per_group + is_v * KV_HEADS + hd))

    groups = range(len(DIL_GROUPS))
    in_specs = ([q_spec(g) for g in groups]
                + [kv_spec(g, v) for g in groups for v in (0, 1)]
                + [halo_spec(g, v) for g in groups for v in (0, 1)])
    grid = (S // SPAN, KV_HEADS)
    side_in, side_out, side_shape = side.plan(grid)
    return pl.pallas_call(
        _dilated_attn_kernel,
        out_shape=(jax.ShapeDtypeStruct((S, Q_HEADS * HEAD_DIM), BF16), side_shape),
        grid=grid,
        in_specs=in_specs + [side_in],
        out_specs=(pl.BlockSpec((SPAN, qw), lambda i, hd: (i, hd)), side_out),
        scratch_shapes=[pltpu.VMEM((2, rep, SPAN, LANES), F32),
                        pltpu.VMEM((2, rep, SPAN, LANES), F32),
                        pltpu.VMEM((2, rep * B, 2 * B), F32)],
        compiler_params=_params(("arbitrary", "arbitrary")),
        name="dilated_attn",
    )(*([q_all] * 3 + [kv_all] * 12 + [side.w]))


def _final_norm_kernel(x_ref, g_ref, o_ref):
    x = x_ref[...]
    ms = jnp.mean(x * x, axis=-1, keepdims=True)
    o_ref[...] = x * lax.rsqrt(ms + EPS) * g_ref[...]


def _final_norm(x, g, tm=512):
    S, D = x.shape
    return pl.pallas_call(
        _final_norm_kernel,
        out_shape=jax.ShapeDtypeStruct((S, D), F32),
        grid=(S // tm,),
        in_specs=[pl.BlockSpec((tm, D), lambda i: (i, 0)), pl.BlockSpec((1, D), lambda i: (0, 0))],
        out_specs=pl.BlockSpec((tm, D), lambda i: (i, 0)),
        compiler_params=_params(("parallel",)),
        name="final_norm",
    )(x, g)


def kernel(x, c, positions, ada_w, ada_b, norm_g, ffn_w_in, ffn_w_out, ret_w_in, ret_w_out, kv_norm_g, kv_ada_w, kv_ada_b, kv_w, attn_w_q, attn_w_out, final_norm_g):
    B, S, D = x.shape
    assert B == 1 and S % SPAN == 0
    assert all(w // d == ATTN_BLOCK for w, d in DIL_GROUPS) and DIL_GROUPS[0][1] == 1
    xs = x.reshape(S, D)

    mod = _ada_matvec(c, ada_w, ada_b[:, None, :])
    kv_mod = _ada_matvec(c, kv_ada_w[None], kv_ada_b[None, None, :])[0]

    def mod_vec(layer, idx):
        return mod[layer, :, idx * D:(idx + 1) * D]

    inv_freq_ret = 1.0 / (RET_THETA ** jnp.linspace(0.0, 1.0, RET_DK // 2, dtype=F32))
    inv_freq_attn = ROPE_THETA ** (-jnp.arange(0, ROT_DIM, 2, dtype=F32) / ROT_DIM)
    gap = jnp.zeros((ROT_PARTNER - ROT_DIM // 2,), F32)
    inv_freq_attn = jnp.concatenate([inv_freq_attn, gap, inv_freq_attn, gap])
    pos_col = positions.reshape(S, 1).astype(F32)
    cos_r, sin_r, cos_a, sin_a = _rope_tables(pos_col, inv_freq_ret[None, :], inv_freq_attn[None, :])

    w_ret_in = _prep_weight(ret_w_in, bw=RET_DK, src_block=_ret_head_src_block)
    qkvg, w_ret_out = _ret_in_proj(xs, norm_g[0, 0][None], mod_vec(0, 0), mod_vec(0, 1), w_ret_in, cos_r, sin_r,
                                   _SideCast(ret_w_out))
    log_g = jnp.log1p(-(2.0 ** (-5.0 - jnp.arange(RET_HEADS, dtype=F32))))
    y, w_ffn_in0 = _retention(qkvg, log_g, _SideCast(ffn_w_in, 0))
    xs, w_kv = _out_proj(y, w_ret_out, xs, mod_vec(0, 2), _SideCast(kv_w[None], pair_period=2))
    hid, w_ffn_out0 = _ffn_in_proj(xs, norm_g[0, 1][None], mod_vec(0, 3), mod_vec(0, 4), w_ffn_in0,
                                   _SideCast(ffn_w_out, 0))
    xs, w_q = _out_proj(hid, w_ffn_out0, xs, mod_vec(0, 5), _SideCast(attn_w_q, pair_period=1))

    kv_all, w_attn_out = _span_proj(xs, kv_norm_g[None], kv_mod[:, :D], kv_mod[:, D:], w_kv, cos_a, sin_a,
                                    _SideCast(attn_w_out), out_scale=1.0, rope_cols=PAIR_TILE)

    q_all, w_ffn_in1 = _span_proj(xs, norm_g[1, 0][None], mod_vec(1, 0), mod_vec(1, 1), w_q, cos_a, sin_a,
                                  _SideCast(ffn_w_in, 1), out_scale=HEAD_DIM ** -0.5 * LOG2E, rope_cols=SPAN_TN)
    attn, w_ffn_out1 = _dilated_attention(q_all, kv_all, _SideCast(ffn_w_out, 1))
    xs = _out_proj(attn, w_attn_out, xs, mod_vec(1, 2))
    hid = _ffn_in_proj(xs, norm_g[1, 1][None], mod_vec(1, 3), mod_vec(1, 4), w_ffn_in1)
    xs = _out_proj(hid, w_ffn_out1, xs, mod_vec(1, 5))

    return _final_norm(xs, final_norm_g[None]).reshape(B, S, D)
```

```python
import functools

import jax
import jax.numpy as jnp
from jax import lax
from jax.experimental import pallas as pl
from jax.experimental.pallas import tpu as pltpu

F32 = jnp.float32
BF16 = jnp.bfloat16

EPS = 1e-6
RET_HEADS = 8
RET_DK = 256
RET_DV = 512
RET_CHUNK = 128
RET_KERNEL_CHUNK = 256
RET_HEADS_PER_STEP = 2
RET_THETA = 10000.0
DIL_GROUPS = ((128, 1), (512, 4), (2048, 16))
HEAD_DIM = 128
Q_HEADS = 16
KV_HEADS = 4
ROT_DIM = 32
ROPE_THETA = 500000.0
ATTN_BLOCK = 128
N_MOD = 6

LANES = 128
VMEM_LIMIT = 56 * 1024 * 1024
SPAN = ATTN_BLOCK * max(d for _, d in DIL_GROUPS)
ROT_PARTNER = LANES // 2
MASKED = -1e30
LOG2E = 1.4426950408889634


def _params(sem, vmem=VMEM_LIMIT):
    return pltpu.CompilerParams(dimension_semantics=sem, vmem_limit_bytes=vmem)


def _silu(x):
    return x / (1.0 + jnp.exp(-x))


def _matvec_kernel(c_ref, w_ref, b_ref, o_ref):
    c = c_ref[...]
    ca = jnp.broadcast_to(_silu(c), (8, c.shape[1])).astype(BF16)
    acc = jnp.dot(ca, w_ref[...].astype(BF16), preferred_element_type=F32)
    o_ref[...] = acc[0:1, :] + b_ref[...]


def _ada_matvec(c, w, b, tn=1024):
    L, D, N = w.shape
    return pl.pallas_call(
        _matvec_kernel,
        out_shape=jax.ShapeDtypeStruct((L, 1, N), F32),
        grid=(L, N // tn),
        in_specs=[
            pl.BlockSpec((1, D), lambda l, j: (0, 0)),
            pl.BlockSpec((None, D, tn), lambda l, j: (l, 0, j)),
            pl.BlockSpec((None, 1, tn), lambda l, j: (l, 0, j)),
        ],
        out_specs=pl.BlockSpec((None, 1, tn), lambda l, j: (l, 0, j)),
        compiler_params=_params(("parallel", "parallel")),
        name="ada_matvec",
    )(c, w, b)


def _rope_block(pos_ref, f_ref, cos_ref, sin_ref, paired):
    ang = pos_ref[...] * f_ref[...]
    cos_ref[...] = jnp.cos(ang)
    sin = jnp.sin(ang)
    if paired:
        lane = lax.broadcasted_iota(jnp.int32, ang.shape, 1)
        sin = jnp.where(lane < ROT_PARTNER, -sin, sin)
    sin_ref[...] = sin


def _rope_table_kernel(pos_ref, fr_ref, fa_ref, cr_ref, sr_ref, ca_ref, sa_ref):
    _rope_block(pos_ref, fr_ref, cr_ref, sr_ref, paired=False)
    _rope_block(pos_ref, fa_ref, ca_ref, sa_ref, paired=True)


def _rope_tables(pos_col, inv_freq_ret, inv_freq_attn, tb=1024):
    S = pos_col.shape[0]
    tab = jax.ShapeDtypeStruct((S, LANES), F32)
    row = pl.BlockSpec((tb, LANES), lambda i: (i, 0))
    vec = pl.BlockSpec((1, LANES), lambda i: (0, 0))
    return pl.pallas_call(
        _rope_table_kernel,
        out_shape=(tab,) * 4,
        grid=(S // tb,),
        in_specs=[pl.BlockSpec((tb, 1), lambda i: (i, 0)), vec, vec],
        out_specs=(row,) * 4,
        compiler_params=_params(("parallel",)),
        name="rope_tables",
    )(pos_col, inv_freq_ret, inv_freq_attn)


PAIR_TILE = KV_HEADS * HEAD_DIM


def _cast_block(w_ref, o_ref, pair_period=0):
    if pair_period == 0:
        o_ref[...] = w_ref[...].astype(o_ref.dtype)
        return
    half = ROT_DIM // 2
    shift = ROT_PARTNER - half
    for ct in range(w_ref.shape[1] // PAIR_TILE):
        if ct % pair_period != 0:
            cols = slice(ct * PAIR_TILE, (ct + 1) * PAIR_TILE)
            o_ref[:, cols] = w_ref[:, cols].astype(o_ref.dtype)
            continue
        for hh in range(PAIR_TILE // HEAD_DIM):
            cols = slice(ct * PAIR_TILE + hh * HEAD_DIM, ct * PAIR_TILE + (hh + 1) * HEAD_DIM)
            x = w_ref[:, cols]
            lane = lax.broadcasted_iota(jnp.int32, x.shape, 1)
            from_hi = (lane >= half) & (lane < ROT_DIM)
            from_lo = (lane >= ROT_PARTNER) & (lane < ROT_PARTNER + half)
            y = jnp.where(from_hi, pltpu.roll(x, HEAD_DIM - shift, 1), jnp.where(from_lo, pltpu.roll(x, shift, 1), x))
            o_ref[:, cols] = y.astype(o_ref.dtype)


def _prep_weight_kernel(w_ref, o_ref):
    _cast_block(w_ref, o_ref)


def _prep_weight(w, layer=0, *, bw=512, src_block=None):
    _, K, N = w.shape
    src = src_block if src_block is not None else (lambda j: j)
    return pl.pallas_call(
        _prep_weight_kernel,
        out_shape=jax.ShapeDtypeStruct((K, N), BF16),
        grid=(N // bw,),
        in_specs=[pl.BlockSpec((None, K, bw), lambda j: (layer, 0, src(j)))],
        out_specs=pl.BlockSpec((K, bw), lambda j: (0, j)),
        compiler_params=_params(("parallel",)),
        name="prep_weight",
    )(w)


class _SideCast:
    def __init__(self, w, layer=0, pair_period=0):
        self.w, self.layer, self.pair_period = w, layer, pair_period

    def plan(self, grid):
        _, K, N = self.w.shape
        n_steps, nj, layer = grid[0] * grid[1], grid[1], self.layer
        bf16_rows = 16
        n_chunks = max(n for n in range(1, n_steps + 1) if K % (n * bf16_rows) == 0)
        rows = K // n_chunks
        chunk = lambda i, j: jnp.minimum(i * nj + j, n_chunks - 1)
        return (pl.BlockSpec((None, rows, N), lambda i, j: (layer, chunk(i, j), 0)),
                pl.BlockSpec((rows, N), lambda i, j: (chunk(i, j), 0)),
                jax.ShapeDtypeStruct((K, N), BF16))


NORM_ROWS = 128


def _norm_modulate_into(x_ref, g_ref, sh_ref, sc_ref, h_ref, row0=0):
    a = g_ref[...] * (1.0 + sc_ref[...])
    b = sh_ref[...]

    def body(c, carry):
        off = pl.multiple_of(c * NORM_ROWS, NORM_ROWS)
        xc = x_ref[pl.ds(off, NORM_ROWS), :]
        ms = jnp.mean(xc * xc, axis=-1, keepdims=True)
        h_ref[pl.ds(pl.multiple_of(row0 + off, NORM_ROWS), NORM_ROWS), :] = (
            xc * lax.rsqrt(ms + EPS) * a + b).astype(BF16)
        return carry

    lax.fori_loop(0, x_ref.shape[0] // NORM_ROWS, body, 0)


RET_HEAD_COLS = 2 * RET_DK + 2 * RET_DV


def _ret_head_src_block(ob):
    per_head = RET_HEAD_COLS // RET_DK
    vblk = RET_DV // RET_DK
    h, p = ob // per_head, ob % per_head
    q_src = h
    k_src = RET_HEADS + h
    v_src = 2 * RET_HEADS + vblk * h + (p - 2)
    g_src = (2 + vblk) * RET_HEADS + vblk * h + (p - 2 - vblk)
    return jnp.where(p == 0, q_src, jnp.where(p == 1, k_src, jnp.where(p < 2 + vblk, v_src, g_src)))


def _ret_in_kernel(x_ref, g_ref, sh_ref, sc_ref, w_ref, cos_ref, sin_ref, sw_ref, o_ref, so_ref, h_ref):
    @pl.when(pl.program_id(1) == 0)
    def _():
        _norm_modulate_into(x_ref, g_ref, sh_ref, sc_ref, h_ref)

    _cast_block(sw_ref, so_ref)
    h = h_ref[...]
    half = RET_DK // 2
    v0, g0 = 2 * RET_DK, 2 * RET_DK + RET_DV

    qk = jnp.dot(h, w_ref[:, :v0], preferred_element_type=F32)
    for lo, scale in ((0, 1.0), (RET_DK, RET_DK ** -0.5)):
        cos = cos_ref[...] * scale
        sin = sin_ref[...] * scale
        a1 = qk[:, lo:lo + half]
        a2 = qk[:, lo + half:lo + RET_DK]
        o_ref[:, lo:lo + half] = (a1 * cos - a2 * sin).astype(o_ref.dtype)
        o_ref[:, lo + half:lo + RET_DK] = (a2 * cos + a1 * sin).astype(o_ref.dtype)

    gate = jnp.dot(h, w_ref[:, g0:], preferred_element_type=F32)
    o_ref[:, g0:] = _silu(gate).astype(o_ref.dtype)
    v = jnp.dot(h, w_ref[:, v0:g0], preferred_element_type=F32)
    o_ref[:, v0:g0] = v.astype(o_ref.dtype)


def _ret_in_proj(x, g, sh, sc, w, cos, sin, side, tm=1024):
    S, D = x.shape
    N = w.shape[1]
    tn = RET_HEAD_COLS
    grid = (S // tm, N // tn)
    vec = pl.BlockSpec((1, D), lambda i, j: (0, 0))
    tab = pl.BlockSpec((tm, LANES), lambda i, j: (i, 0))
    side_in, side_out, side_shape = side.plan(grid)
    return pl.pallas_call(
        _ret_in_kernel,
        out_shape=(jax.ShapeDtypeStruct((S, N), BF16), side_shape),
        grid=grid,
        in_specs=[pl.BlockSpec((tm, D), lambda i, j: (i, 0)), vec, vec, vec,
                  pl.BlockSpec((D, tn), lambda i, j: (0, j)), tab, tab, side_in],
        out_specs=(pl.BlockSpec((tm, tn), lambda i, j: (i, j)), side_out),
        scratch_shapes=[pltpu.VMEM((tm, D), BF16)],
        compiler_params=_params(("arbitrary", "arbitrary")),
        name="ret_in_proj",
    )(x, g, sh, sc, w, cos, sin, side.w)


SPAN_NORM_STEPS = 4
SPAN_SUB = 256
SPAN_TN = 1024
SPAN_ACC_BUFS = 2


def _span_chunks(d):
    B = ATTN_BLOCK
    out = []
    for s in range(SPAN // (B * d)):
        for r in range(d):
            u = s * d + r
            out.append((pl.ds(s * B * d + r, B, stride=d) if d > 1 else pl.ds(u * B, B), pl.ds(u * B, B)))
    return out


CHEAP_STRIDE = 4


def _presplit_rows(src_ref, dst_ref):
    q = src_ref.shape[0] // CHEAP_STRIDE
    for r in range(CHEAP_STRIDE):
        dst_ref[r * q:(r + 1) * q, :] = src_ref[pl.ds(r, q, stride=CHEAP_STRIDE), :]


def _span_chunks_presplit(d):
    B = ATTN_BLOCK
    assert d % CHEAP_STRIDE == 0 and B * d == SPAN
    q = SPAN // CHEAP_STRIDE
    return [(pl.ds((r % CHEAP_STRIDE) * q + r // CHEAP_STRIDE, B, stride=d // CHEAP_STRIDE), pl.ds(r * B, B))
            for r in range(d)]


def _span_proj_kernel(x_ref, g_ref, sh_ref, sc_ref, w_ref, c_ref, s_ref, sw_ref, o_ref, so_ref,
                      h_ref, tab_ref, *acc_refs, out_scale, steps_per_group, rope_cols):
    j = pl.program_id(1)
    jm = j - SPAN_NORM_STEPS

    @pl.when(jm < 0)
    def _():
        _cast_block(sw_ref, so_ref)
        _norm_modulate_into(x_ref, g_ref, sh_ref, sc_ref, h_ref, row0=j * (SPAN // SPAN_NORM_STEPS))

    def run(d):
        chunks = _span_chunks(d)
        data_chunks = _span_chunks_presplit(d) if d > CHEAP_STRIDE else chunks

        @pl.when(jm % steps_per_group == 0)
        def _():
            for src, dst in chunks:
                tab_ref[0, dst, :] = c_ref[src, :] * out_scale
                tab_ref[1, dst, :] = s_ref[src, :] * out_scale

        _cast_block(sw_ref, so_ref)
        n_slab = SPAN_SUB // LANES
        for t in range(w_ref.shape[1] // SPAN_SUB):
            col0 = t * SPAN_SUB
            acc = jnp.dot(h_ref[...], w_ref[:, col0:col0 + SPAN_SUB], preferred_element_type=F32)
            buf = (t % SPAN_ACC_BUFS) * n_slab
            slabs = acc_refs[buf:buf + n_slab]
            for c in range(n_slab):
                slabs[c][...] = acc[:, c * LANES:(c + 1) * LANES]
            if d > CHEAP_STRIDE:
                split = acc_refs[SPAN_ACC_BUFS * n_slab + buf:SPAN_ACC_BUFS * n_slab + buf + n_slab]
                for c in range(n_slab):
                    _presplit_rows(slabs[c], split[c])
                slabs = split
            rope = col0 < rope_cols
            for src, dst in data_chunks:
                if rope:
                    ct = tab_ref[0, dst, :]
                    st = tab_ref[1, dst, :]
                for c in range(n_slab):
                    xr = slabs[c][src, :]
                    y = xr * ct + pltpu.roll(xr, ROT_PARTNER, 1) * st if rope else xr
                    o_ref[dst, col0 + c * LANES:col0 + (c + 1) * LANES] = y.astype(o_ref.dtype)

    for gi, (_, d) in enumerate(DIL_GROUPS):
        pl.when((jm >= 0) & (jm // steps_per_group == gi))(functools.partial(run, d))


def _span_proj(x, g, sh, sc, w, c, s, side, *, out_scale, rope_cols):
    S, D = x.shape
    N = w.shape[1]
    tm, tn, ns = SPAN, SPAN_TN, SPAN_NORM_STEPS
    n_mm = N // tn
    grid = (S // tm, ns + n_mm)
    vec = pl.BlockSpec((1, D), lambda i, j: (0, 0))
    tab = pl.BlockSpec((tm, LANES), lambda i, j: (i, 0))
    mm_step = lambda j: jnp.maximum(j - ns, 0)
    side_in, side_out, side_shape = side.plan(grid)
    return pl.pallas_call(
        functools.partial(_span_proj_kernel, out_scale=out_scale,
                          steps_per_group=n_mm // len(DIL_GROUPS), rope_cols=rope_cols),
        out_shape=(jax.ShapeDtypeStruct((S, N), BF16), side_shape),
        grid=grid,
        in_specs=[pl.BlockSpec((tm // ns, D), lambda i, j: (i * ns + jnp.minimum(j, ns - 1), 0)), vec, vec, vec,
                  pl.BlockSpec((D, tn), lambda i, j: (0, mm_step(j))), tab, tab, side_in],
        out_specs=(pl.BlockSpec((tm, tn), lambda i, j: (i, mm_step(j))), side_out),
        scratch_shapes=([pltpu.VMEM((tm, D), BF16), pltpu.VMEM((2, tm, LANES), F32)]
                        + [pltpu.VMEM((tm, LANES), F32)] * (2 * SPAN_ACC_BUFS * SPAN_SUB // LANES)),
        compiler_params=_params(("arbitrary", "arbitrary")),
        name="span_proj",
    )(x, g, sh, sc, w, c, s, side.w)


def _with_side(side, grid, in_specs, out_spec, out_shape, operands):
    if side is None:
        return in_specs, out_spec, out_shape, operands
    side_in, side_out, side_shape = side.plan(grid)
    return in_specs + [side_in], (out_spec, side_out), (out_shape, side_shape), operands + (side.w,)


def _ffn_in_kernel(x_ref, g_ref, sh_ref, sc_ref, wg_ref, wu_ref, *rest, side_pair):
    if side_pair is None:
        o_ref, h_ref = rest
    else:
        sw_ref, o_ref, so_ref, h_ref = rest

    @pl.when(pl.program_id(1) == 0)
    def _():
        _norm_modulate_into(x_ref, g_ref, sh_ref, sc_ref, h_ref)

    if side_pair is not None:
        _cast_block(sw_ref, so_ref, side_pair)
    h = h_ref[...]
    gate = jnp.dot(h, wg_ref[...], preferred_element_type=F32)
    up = jnp.dot(h, wu_ref[...], preferred_element_type=F32)
    o_ref[...] = (_silu(gate) * up).astype(o_ref.dtype)


def _ffn_in_proj(x, g, sh, sc, w, side=None, tm=1024, tn=512):
    S, D = x.shape
    F = w.shape[1] // 2
    nj = F // tn
    grid = (S // tm, nj)
    vec = pl.BlockSpec((1, D), lambda i, j: (0, 0))
    in_specs, out_specs, out_shape, operands = _with_side(
        side, grid,
        [pl.BlockSpec((tm, D), lambda i, j: (i, 0)), vec, vec, vec,
         pl.BlockSpec((D, tn), lambda i, j: (0, j)),
         pl.BlockSpec((D, tn), lambda i, j: (0, j + nj))],
        pl.BlockSpec((tm, tn), lambda i, j: (i, j)),
        jax.ShapeDtypeStruct((S, F), BF16),
        (x, g, sh, sc, w, w))
    return pl.pallas_call(
        functools.partial(_ffn_in_kernel, side_pair=None if side is None else side.pair_period),
        out_shape=out_shape,
        grid=grid,
        in_specs=in_specs,
        out_specs=out_specs,
        scratch_shapes=[pltpu.VMEM((tm, D), BF16)],
        compiler_params=_params(("arbitrary", "arbitrary")),
        name="ffn_in_proj",
    )(*operands)


def _out_proj_kernel(a_ref, w_ref, r_ref, gt_ref, *rest, side_pair):
    if side_pair is None:
        (o_ref,) = rest
    else:
        sw_ref, o_ref, so_ref = rest
        _cast_block(sw_ref, so_ref, side_pair)
    acc = jnp.dot(a_ref[...], w_ref[...], preferred_element_type=F32)
    o_ref[...] = r_ref[...] + gt_ref[...] * acc


def _out_proj(a, w, resid, gate, side=None, tm=1024, tn=512):
    S, K = a.shape
    N = w.shape[1]
    grid = (S // tm, N // tn)
    in_specs, out_specs, out_shape, operands = _with_side(
        side, grid,
        [pl.BlockSpec((tm, K), lambda i, j: (i, 0)),
         pl.BlockSpec((K, tn), lambda i, j: (0, j)),
         pl.BlockSpec((tm, tn), lambda i, j: (i, j)),
         pl.BlockSpec((1, tn), lambda i, j: (0, j))],
        pl.BlockSpec((tm, tn), lambda i, j: (i, j)),
        jax.ShapeDtypeStruct((S, N), F32),
        (a, w, resid, gate))
    return pl.pallas_call(
        functools.partial(_out_proj_kernel, side_pair=None if side is None else side.pair_period),
        out_shape=out_shape,
        grid=grid,
        in_specs=in_specs,
        out_specs=out_specs,
        compiler_params=_params(("arbitrary", "arbitrary")),
        name="out_proj",
    )(*operands)


def _retention_kernel(lg_ref, x_ref, sw_ref, o_ref, so_ref, r_ref):
    _cast_block(sw_ref, so_ref)

    @pl.when(pl.program_id(1) == 0)
    def _():
        r_ref[...] = jnp.zeros_like(r_ref)

    C = RET_KERNEL_CHUNK
    k0, v0, g0 = RET_DK, 2 * RET_DK, 2 * RET_DK + RET_DV
    ri = lax.broadcasted_iota(jnp.int32, (C, C), 0)
    ci = lax.broadcasted_iota(jnp.int32, (C, C), 1)
    diff = (ri - ci).astype(F32)
    idx = lax.broadcasted_iota(jnp.int32, (C, 1), 0).astype(F32)
    decays = []
    for hh in range(RET_HEADS_PER_STEP):
        lg = lg_ref[pl.program_id(0) * RET_HEADS_PER_STEP + hh]
        decays.append((jnp.where(diff >= 0, jnp.exp(lg * jnp.maximum(diff, 0.0)), 0.0),
                       jnp.exp(lg * (idx + 1.0)),
                       jnp.exp(lg * (C - 1.0 - idx)),
                       jnp.exp(jnp.full((1, 1), lg * C, F32))))

    def body(c, carry):
        rows = pl.ds(pl.multiple_of(c * C, C), C)
        for hh, (dmask, q_decay, k_decay, chunk_decay) in enumerate(decays):
            x0 = hh * RET_HEAD_COLS
            qc = x_ref[rows, x0:x0 + k0]
            kc = x_ref[rows, x0 + k0:x0 + v0]
            vc = x_ref[rows, x0 + v0:x0 + g0]
            state = r_ref[hh]
            scores = lax.dot_general(qc, kc, (((1,), (1,)), ((), ())), preferred_element_type=F32) * dmask
            o = (jnp.dot(scores.astype(BF16), vc, preferred_element_type=F32)
                 + jnp.dot(qc, state.astype(BF16), preferred_element_type=F32) * q_decay)
            kd = (kc.astype(F32) * k_decay).astype(BF16)
            r_ref[hh] = state * chunk_decay + lax.dot_general(
                kd, vc, (((0,), (0,)), ((), ())), preferred_element_type=F32)
            y = o * lax.rsqrt(jnp.mean(o * o, axis=-1, keepdims=True) + EPS)
            gate = x_ref[rows, x0 + g0:x0 + RET_HEAD_COLS].astype(F32)
            o_ref[rows, hh * RET_DV:(hh + 1) * RET_DV] = (gate * y).astype(o_ref.dtype)
        return carry

    lax.fori_loop(0, x_ref.shape[0] // C, body, 0, unroll=2)


def _retention(qkvg, log_g, side, tb=2048):
    S = qkvg.shape[0]
    hps = RET_HEADS_PER_STEP
    grid = (RET_HEADS // hps, S // tb)
    side_in, side_out, side_shape = side.plan(grid)
    return pl.pallas_call(
        _retention_kernel,
        out_shape=(jax.ShapeDtypeStruct((S, RET_HEADS * RET_DV), BF16), side_shape),
        grid=grid,
        in_specs=[pl.BlockSpec(memory_space=pltpu.SMEM),
                  pl.BlockSpec((tb, hps * RET_HEAD_COLS), lambda h, t: (t, h)), side_in],
        out_specs=(pl.BlockSpec((tb, hps * RET_DV), lambda h, t: (t, h)), side_out),
        scratch_shapes=[pltpu.VMEM((hps, RET_DK, RET_DV), F32)],
        compiler_params=_params(("arbitrary", "arbitrary")),
        name="retention",
    )(log_g, qkvg, side.w)


UNIT_UNROLL = 16


def _attn_unit(q_ref, row, k_own, v_own, k_prev, v_prev, bias):
    rep = Q_HEADS // KV_HEADS
    B = ATTN_BLOCK
    q4 = jnp.concatenate([q_ref[pl.ds(row, B), r * HEAD_DIM:(r + 1) * HEAD_DIM] for r in range(rep)], axis=0)
    k_cat = jnp.concatenate([k_prev, k_own], axis=0)
    v_cat = jnp.concatenate([v_prev, v_own], axis=0)
    v_ext = jnp.concatenate([v_cat, jnp.ones_like(v_cat)], axis=1)
    s = lax.dot_general(q4, k_cat, (((1,), (1,)), ((), ())), preferred_element_type=F32) + bias
    m = jnp.max(s, axis=-1, keepdims=True)
    p = jnp.exp2(s - m).astype(BF16)
    ov = jnp.dot(p, v_ext, preferred_element_type=F32)
    l = ov[:, HEAD_DIM:]
    return ov[:, :HEAD_DIM] / l, m + jnp.log2(l)


def _dilated_attn_kernel(q0_ref, q1_ref, q2_ref, k0_ref, v0_ref, k1_ref, v1_ref, k2_ref, v2_ref,
                         hk0_ref, hv0_ref, hk1_ref, hv1_ref, hk2_ref, hv2_ref, sw_ref,
                         o_ref, so_ref, onat_ref, lnat_ref, bias_ref):
    _cast_block(sw_ref, so_ref)
    i = pl.program_id(0)
    B = ATTN_BLOCK
    rep = Q_HEADS // KV_HEADS
    n_units = SPAN // B
    d1, d2 = DIL_GROUPS[1][1], DIL_GROUPS[2][1]

    @pl.when((i == 0) & (pl.program_id(1) == 0))
    def _():
        ri = lax.broadcasted_iota(jnp.int32, (rep * B, 2 * B), 0) % B
        ci = lax.broadcasted_iota(jnp.int32, (rep * B, 2 * B), 1)
        own = (ci >= B) & (ci - B <= ri)
        prev = (ci < B) & (ci >= ri)
        bias_ref[0] = jnp.where(own | prev, 0.0, MASKED)
        bias_ref[1] = jnp.where(own, 0.0, MASKED)

    def scatter(slot, dst, o, lse):
        for hq in range(rep):
            onat_ref[slot, hq, dst, :] = o[hq * B:(hq + 1) * B, :]
            lnat_ref[slot, hq, dst, :] = lse[hq * B:(hq + 1) * B, :]

    def widest_body(r, carry):
        row = pl.multiple_of(r * B, B)
        rows = pl.ds(row, B)
        bias = bias_ref[(i == 0).astype(jnp.int32)]
        o, lse = _attn_unit(q2_ref, row, k2_ref[rows, :], v2_ref[rows, :], hk2_ref[rows, :], hv2_ref[rows, :], bias)
        scatter(1, pl.ds(r, B, stride=d2), o, lse)
        return carry

    lax.fori_loop(0, n_units, widest_body, 0, unroll=UNIT_UNROLL)

    def middle_body(u, carry):
        s = u // d1
        r = u % d1
        row = pl.multiple_of(u * B, B)
        rows = pl.ds(row, B)
        prow = pl.ds(pl.multiple_of(jnp.maximum(u - d1, 0) * B, B), B)
        hrow = pl.ds(pl.multiple_of(r * B, B), B)
        first = s == 0
        k_prev = jnp.where(first, hk1_ref[hrow, :], k1_ref[prow, :])
        v_prev = jnp.where(first, hv1_ref[hrow, :], v1_ref[prow, :])
        bias = bias_ref[(first & (i == 0)).astype(jnp.int32)]
        o, lse = _attn_unit(q1_ref, row, k1_ref[rows, :], v1_ref[rows, :], k_prev, v_prev, bias)
        scatter(0, pl.ds(s * (B * d1) + r, B, stride=d1), o, lse)
        return carry

    lax.fori_loop(0, n_units, middle_body, 0, unroll=UNIT_UNROLL)

    def dense_body(b, carry):
        row = pl.multiple_of(b * B, B)
        rows = pl.ds(row, B)
        prow = pl.ds(pl.multiple_of(jnp.maximum(b - 1, 0) * B, B), B)
        first = b == 0
        k_prev = jnp.where(first, hk0_ref[...], k0_ref[prow, :])
        v_prev = jnp.where(first, hv0_ref[...], v0_ref[prow, :])
        bias = bias_ref[(first & (i == 0)).astype(jnp.int32)]
        o0, l0 = _attn_unit(q0_ref, row, k0_ref[rows, :], v0_ref[rows, :], k_prev, v_prev, bias)
        for hq in range(rep):
            a0 = o0[hq * B:(hq + 1) * B, :]
            e0 = l0[hq * B:(hq + 1) * B, :]
            e1 = lnat_ref[0, hq, rows, :]
            e2 = lnat_ref[1, hq, rows, :]
            mx = jnp.maximum(jnp.maximum(e0, e1), e2)
            w0, w1, w2 = jnp.exp2(e0 - mx), jnp.exp2(e1 - mx), jnp.exp2(e2 - mx)
            merged = (w0 * a0 + w1 * onat_ref[0, hq, rows, :] + w2 * onat_ref[1, hq, rows, :]) / (w0 + w1 + w2)
            o_ref[rows, hq * HEAD_DIM:(hq + 1) * HEAD_DIM] = merged.astype(o_ref.dtype)
        return carry

    lax.fori_loop(0, n_units, dense_body, 0, unroll=UNIT_UNROLL)


def _dilated_attention(q_all, kv_all, side):
    S = q_all.shape[0]
    rep = Q_HEADS // KV_HEADS
    qw = rep * HEAD_DIM
    per_group = 2 * KV_HEADS
    B = ATTN_BLOCK

    def q_spec(g):
        return pl.BlockSpec((SPAN, qw), lambda i, hd: (i, g * KV_HEADS + hd))

    def kv_spec(g, is_v):
        return pl.BlockSpec((SPAN, HEAD_DIM), lambda i, hd: (i, g * per_group + is_v * KV_HEADS + hd))

    def halo_spec(g, is_v):
        rows = B * DIL_GROUPS[g][1]
        n = SPAN // rows
        return pl.BlockSpec((rows, HEAD_DIM),
                            lambda i, hd: (jnp.maximum(i * n - 1, 0), g * per_group + is_v * KV_HEADS + hd))

    groups = range(len(DIL_GROUPS))
    in_specs = ([q_spec(g) for g in groups]
                + [kv_spec(g, v) for g in groups for v in (0, 1)]
                + [halo_spec(g, v) for g in groups for v in (0, 1)])
    grid = (S // SPAN, KV_HEADS)
    side_in, side_out, side_shape = side.plan(grid)
    return pl.pallas_call(
        _dilated_attn_kernel,
        out_shape=(jax.ShapeDtypeStruct((S, Q_HEADS * HEAD_DIM), BF16), side_shape),
        grid=grid,
        in_specs=in_specs + [side_in],
        out_specs=(pl.BlockSpec((SPAN, qw), lambda i, hd: (i, hd)), side_out),
        scratch_shapes=[pltpu.VMEM((2, rep, SPAN, LANES), F32),
                        pltpu.VMEM((2, rep, SPAN, LANES), F32),
                        pltpu.VMEM((2, rep * B, 2 * B), F32)],
        compiler_params=_params(("arbitrary", "arbitrary")),
        name="dilated_attn",
    )(*([q_all] * 3 + [kv_all] * 12 + [side.w]))


def _final_norm_kernel(x_ref, g_ref, o_ref):
    x = x_ref[...]
    ms = jnp.mean(x * x, axis=-1, keepdims=True)
    o_ref[...] = x * lax.rsqrt(ms + EPS) * g_ref[...]


def _final_norm(x, g, tm=512):
    S, D = x.shape
    return pl.pallas_call(
        _final_norm_kernel,
        out_shape=jax.ShapeDtypeStruct((S, D), F32),
        grid=(S // tm,),
        in_specs=[pl.BlockSpec((tm, D), lambda i: (i, 0)), pl.BlockSpec((1, D), lambda i: (0, 0))],
        out_specs=pl.BlockSpec((tm, D), lambda i: (i, 0)),
        compiler_params=_params(("parallel",)),
        name="final_norm",
    )(x, g)


def kernel(x, c, positions, ada_w, ada_b, norm_g, ffn_w_in, ffn_w_out, ret_w_in, ret_w_out, kv_norm_g, kv_ada_w, kv_ada_b, kv_w, attn_w_q, attn_w_out, final_norm_g):
    B, S, D = x.shape
    assert B == 1 and S % SPAN == 0
    assert all(w // d == ATTN_BLOCK for w, d in DIL_GROUPS) and DIL_GROUPS[0][1] == 1
    xs = x.reshape(S, D)

    mod = _ada_matvec(c, ada_w, ada_b[:, None, :])
    kv_mod = _ada_matvec(c, kv_ada_w[None], kv_ada_b[None, None, :])[0]

    def mod_vec(layer, idx):
        return mod[layer, :, idx * D:(idx + 1) * D]

    inv_freq_ret = 1.0 / (RET_THETA ** jnp.linspace(0.0, 1.0, RET_DK // 2, dtype=F32))
    inv_freq_attn = ROPE_THETA ** (-jnp.arange(0, ROT_DIM, 2, dtype=F32) / ROT_DIM)
    gap = jnp.zeros((ROT_PARTNER - ROT_DIM // 2,), F32)
    inv_freq_attn = jnp.concatenate([inv_freq_attn, gap, inv_freq_attn, gap])
    pos_col = positions.reshape(S, 1).astype(F32)
    cos_r, sin_r, cos_a, sin_a = _rope_tables(pos_col, inv_freq_ret[None, :], inv_freq_attn[None, :])

    w_ret_in = _prep_weight(ret_w_in, bw=RET_DK, src_block=_ret_head_src_block)
    qkvg, w_ret_out = _ret_in_proj(xs, norm_g[0, 0][None], mod_vec(0, 0), mod_vec(0, 1), w_ret_in, cos_r, sin_r,
                                   _SideCast(ret_w_out))
    log_g = jnp.log1p(-(2.0 ** (-5.0 - jnp.arange(RET_HEADS, dtype=F32))))
    y, w_ffn_in0 = _retention(qkvg, log_g, _SideCast(ffn_w_in, 0))
    xs, w_kv = _out_proj(y, w_ret_out, xs, mod_vec(0, 2), _SideCast(kv_w[None], pair_period=2))
    hid, w_ffn_out0 = _ffn_in_proj(xs, norm_g[0, 1][None], mod_vec(0, 3), mod_vec(0, 4), w_ffn_in0,
                                   _SideCast(ffn_w_out, 0))
    xs, w_q = _out_proj(hid, w_ffn_out0, xs, mod_vec(0, 5), _SideCast(attn_w_q, pair_period=1))

    kv_all, w_attn_out = _span_proj(xs, kv_norm_g[None], kv_mod[:, :D], kv_mod[:, D:], w_kv, cos_a, sin_a,
                                    _SideCast(attn_w_out), out_scale=1.0, rope_cols=PAIR_TILE)

    q_all, w_ffn_in1 = _span_proj(xs, norm_g[1, 0][None], mod_vec(1, 0), mod_vec(1, 1), w_q, cos_a, sin_a,
                                  _SideCast(ffn_w_in, 1), out_scale=HEAD_DIM ** -0.5 * LOG2E, rope_cols=SPAN_TN)
    attn, w_ffn_out1 = _dilated_attention(q_all, kv_all, _SideCast(ffn_w_out, 1))
    xs = _out_proj(attn, w_attn_out, xs, mod_vec(1, 2))
    hid = _ffn_in_proj(xs, norm_g[1, 1][None], mod_vec(1, 3), mod_vec(1, 4), w_ffn_in1)
    xs = _out_proj(hid, w_ffn_out1, xs, mod_vec(1, 5))

    return _final_norm(xs, final_norm_g[None]).reshape(B, S, D)
```
